```python
import math
import jax, jax.numpy as jnp
from jax import lax
import numpy as np

D_MODEL = 1024
BATCH = 4
SEQ = 8192
DEPTH = 2

N_EVEN = (DEPTH + 1) // 2
N_ODD = DEPTH // 2
DN_ALPHA = (2 * DEPTH) ** 0.25
DN_BETA = (8 * DEPTH) ** -0.25
LN_EPS = 1e-5

RWKV_WIDTH = D_MODEL // 2
RWKV_HEAD_DIM = 64
RWKV_HEADS = RWKV_WIDTH // RWKV_HEAD_DIM
W_LORA = 64
A_LORA = 64
G_LORA = 128
RWKV_GN_EPS = 64e-5
RWKV_COLS = 3 * RWKV_WIDTH + W_LORA + A_LORA + G_LORA
RWKV_SPLITS = [RWKV_WIDTH, 2 * RWKV_WIDTH, 3 * RWKV_WIDTH, 3 * RWKV_WIDTH + W_LORA, 3 * RWKV_WIDTH + W_LORA + A_LORA]

LRU_WIDTH = D_MODEL // 2
LRU_BLOCKS = 8
LRU_BLOCK = LRU_WIDTH // LRU_BLOCKS
LRU_C = 8.0
LRU_CONV = 4
EVEN_IN = RWKV_COLS + 2 * LRU_WIDTH
EVEN_MIX = RWKV_WIDTH + LRU_WIDTH

CONF_WIDTH = D_MODEL // 2
CONF_CONV = 31

SSM_INNER = D_MODEL
SSM_HEAD_DIM = 64
SSM_HEADS = SSM_INNER // SSM_HEAD_DIM
SSM_GROUPS = 2
SSM_HEADS_PER_GROUP = SSM_HEADS // SSM_GROUPS
SSM_STATE = 128
SSM_CONV = 4
SSM_CHUNK = 128
SSM_XBC = SSM_INNER + 2 * SSM_GROUPS * SSM_STATE
ODD_IN = 2 * CONF_WIDTH + SSM_INNER + SSM_XBC + SSM_HEADS
ODD_MIX = CONF_WIDTH + SSM_INNER

N_MEM = 256
XA_HEADS = 4
XA_HEAD_DIM = D_MODEL // XA_HEADS
D_FF = 4 * D_MODEL

kernel_name = 'hybrid_rwkv7_rglru_conformer_ssd_deepnorm'


def _layer_norm(x, g, b):
    xf = x.astype(jnp.float32)
    mu = xf.mean(-1, keepdims=True)
    var = jnp.square(xf - mu).mean(-1, keepdims=True)
    return ((xf - mu) * lax.rsqrt(var + LN_EPS) * g + b).astype(x.dtype)


def _causal_dwconv(x, w, b):
    k_w, ch = w.shape
    y = lax.conv_general_dilated(x, w[:, None, :].astype(x.dtype), window_strides=(1,),
                                 padding=[(k_w - 1, 0)], dimension_numbers=('NWC', 'WIO', 'NWC'),
                                 feature_group_count=ch)
    return y + b


def _token_shift(p):
    return jnp.pad(p, ((0, 0), (1, 0), (0, 0)))[:, :-1]


def _linear_combine(e1, e2):
    a1, b1 = e1
    a2, b2 = e2
    return a1 * a2, a2 * b1 + b2


def _rwkv7_time_mix(p, mu, w0, w2, a0, a2, g2, k_k, k_a, r_k, gn_g, gn_b):
    f32 = jnp.float32
    bsz, seqlen, _ = p.shape
    p = p.astype(f32)
    p = p + (_token_shift(p) - p) * mu
    r, k, v, wd, ad, gd = jnp.split(p, RWKV_SPLITS, axis=-1)
    w = -jax.nn.softplus(-(w0 + jnp.tanh(wd) @ w2)) - 0.5
    decay = jnp.exp(-jnp.exp(w))
    a = jax.nn.sigmoid(a0 + ad @ a2)
    g = jax.nn.sigmoid(gd) @ g2
    heads = lambda t: t.reshape(bsz, seqlen, RWKV_HEADS, RWKV_HEAD_DIM)
    kk = heads(k * k_k)
    kk = kk / jnp.maximum(jnp.linalg.norm(kk, axis=-1, keepdims=True), 1e-12)
    k = k * (1.0 + (a - 1.0) * k_a)
    r_h, k_h, v_h, w_h, a_h = heads(r), heads(k), heads(v), heads(decay), heads(a)
    seq_in = tuple(jnp.moveaxis(t, 1, 0) for t in (r_h, w_h, k_h, v_h, kk, a_h))

    def step(state, inp):
        r_t, w_t, k_t, v_t, kk_t, a_t = inp
        sa = jnp.einsum('bhvk,bhk->bhv', state, -kk_t)
        state = (state * w_t[:, :, None, :] + sa[..., None] * (kk_t * a_t)[:, :, None, :]
                 + v_t[..., None] * k_t[:, :, None, :])
        return state, jnp.einsum('bhvk,bhk->bhv', state, r_t)

    s0 = jnp.zeros((bsz, RWKV_HEADS, RWKV_HEAD_DIM, RWKV_HEAD_DIM), f32)
    _, o = lax.scan(step, s0, seq_in)
    o = jnp.moveaxis(o, 0, 1)
    mean = o.mean(-1, keepdims=True)
    var = jnp.square(o - mean).mean(-1, keepdims=True)
    o = ((o - mean) * lax.rsqrt(var + RWKV_GN_EPS)).reshape(bsz, seqlen, RWKV_WIDTH) * gn_g + gn_b
    bonus = jnp.sum(r_h * k_h * r_k, axis=-1, keepdims=True) * v_h
    return (o + bonus.reshape(bsz, seqlen, RWKV_WIDTH)) * g


def _rglru_block(xb, gb, conv_w, conv_b, wa, ba, wx, bx, lam):
    f32 = jnp.float32
    bsz, seqlen, _ = xb.shape
    xb = _causal_dwconv(xb.astype(f32), conv_w, conv_b)
    xh = xb.reshape(bsz, seqlen, LRU_BLOCKS, LRU_BLOCK)
    gate_r = jax.nn.sigmoid(jnp.einsum('btnd,nde->btne', xh, wa).reshape(bsz, seqlen, LRU_WIDTH) + ba)
    gate_i = jax.nn.sigmoid(jnp.einsum('btnd,nde->btne', xh, wx).reshape(bsz, seqlen, LRU_WIDTH) + bx)
    log_a = -LRU_C * jax.nn.softplus(-lam) * gate_r
    a = jnp.exp(log_a)
    u = jnp.sqrt(-jnp.expm1(2.0 * log_a)) * (gate_i * xb)
    _, h = lax.associative_scan(_linear_combine, (a, u), axis=1)
    return h * jax.nn.gelu(gb.astype(f32))


def _conformer_conv(c, conv_w, conv_b, ln_g, ln_b):
    c = c.astype(jnp.float32)
    val, gate = jnp.split(c, 2, axis=-1)
    u = _causal_dwconv(val * jax.nn.sigmoid(gate), conv_w, conv_b)
    return jax.nn.silu(_layer_norm(u, ln_g, ln_b))


def _mamba2_ssd(z, xbc, dt_raw, conv_w, conv_b, dt_bias, a_log, d_skip, norm_g):
    f32 = jnp.float32
    bsz, seqlen, _ = z.shape
    n_c, L = seqlen // SSM_CHUNK, SSM_CHUNK
    G, HG, P, N = SSM_GROUPS, SSM_HEADS_PER_GROUP, SSM_HEAD_DIM, SSM_STATE
    xbc = jax.nn.silu(_causal_dwconv(xbc.astype(f32), conv_w, conv_b))
    xs, bm, cm = jnp.split(xbc, [SSM_INNER, SSM_INNER + G * N], axis=-1)
    dt = jax.nn.softplus(dt_raw.astype(f32) + dt_bias)
    a_head = -jnp.exp(a_log.astype(f32))
    x_c = xs.reshape(bsz, n_c, L, G, HG, P)
    dt_c = dt.reshape(bsz, n_c, L, G, HG)
    b_c = bm.reshape(bsz, n_c, L, G, N)
    c_c = cm.reshape(bsz, n_c, L, G, N)
    a_cum = jnp.cumsum(dt_c * a_head.reshape(G, HG), axis=2)
    xdt = x_c * dt_c[..., None]
    seg = a_cum[:, :, :, None] - a_cum[:, :, None, :]
    causal = jnp.tril(jnp.ones((L, L), dtype=bool))[None, None, :, :, None, None]
    decay_ls = jnp.exp(jnp.where(causal, seg, -jnp.inf))
    cb = jnp.einsum('bclgn,bcsgn->bclsg', c_c, b_c)
    y_diag = jnp.einsum('bclsgh,bcsghp->bclghp', cb[..., None] * decay_ls, xdt)
    decay_to_end = jnp.exp(a_cum[:, :, -1:] - a_cum)
    states = jnp.einsum('bclgn,bclgh,bclghp->bcghpn', b_c, decay_to_end, xdt)
    chunk_decay = jnp.exp(a_cum[:, :, -1])

    def pass_state(s, inp):
        st, dc = inp
        return s * dc[..., None, None] + st, s

    s0 = jnp.zeros((bsz, G, HG, P, N), f32)
    _, prev = lax.scan(pass_state, s0, (jnp.moveaxis(states, 1, 0), jnp.moveaxis(chunk_decay, 1, 0)))
    prev = jnp.moveaxis(prev, 0, 1)
    y_off = jnp.einsum('bclgn,bcghpn,bclgh->bclghp', c_c, prev, jnp.exp(a_cum))
    y = (y_diag + y_off).reshape(bsz, seqlen, SSM_HEADS, P) + d_skip[:, None] * xs.reshape(bsz, seqlen, SSM_HEADS, P)
    y = y.reshape(bsz, seqlen, SSM_INNER) * jax.nn.silu(z.astype(f32))
    yg = y.reshape(bsz, seqlen, G, SSM_INNER // G)
    yg = yg * lax.rsqrt(jnp.mean(jnp.square(yg), axis=-1, keepdims=True) + LN_EPS)
    return yg.reshape(bsz, seqlen, SSM_INNER) * norm_g


def _mem_cross_attn(h, mem, wq, wk, wv, wo):
    bsz, seqlen, _ = h.shape
    q = (h @ wq).reshape(bsz, seqlen, XA_HEADS, XA_HEAD_DIM)
    k = (mem @ wk).reshape(bsz, mem.shape[1], XA_HEADS, XA_HEAD_DIM)
    v = (mem @ wv).reshape(bsz, mem.shape[1], XA_HEADS, XA_HEAD_DIM)
    s = jnp.einsum('bthd,bmhd->bhtm', q, k).astype(jnp.float32) * (XA_HEAD_DIM ** -0.5)
    pr = jax.nn.softmax(s, axis=-1).astype(v.dtype)
    o = jnp.einsum('bhtm,bmhd->bthd', pr, v).reshape(bsz, seqlen, D_MODEL)
    return o @ wo


def setup_inputs(seed: int = 0) -> dict:
    key = jax.random.key(seed)
    keys = jax.random.split(key, 64)
    counter = iter(range(64))
    nk = lambda: keys[next(counter)]
    f32 = jnp.float32
    nrm = lambda shape, scale: scale * jax.random.normal(nk(), shape, f32)
    unif = lambda shape, lo, hi: jax.random.uniform(nk(), shape, f32, lo, hi)
    d_in = D_MODEL ** -0.5
    inp = {}
    inp['x'] = nrm((BATCH, SEQ, D_MODEL), 1.0)
    inp['mem'] = nrm((BATCH, N_MEM, D_MODEL), 1.0)
    inp['ev_w_in'] = nrm((N_EVEN, D_MODEL, EVEN_IN), d_in)
    inp['ev_mu'] = unif((N_EVEN, RWKV_COLS), 0.0, 1.0)
    inp['ev_w0'] = unif((N_EVEN, RWKV_WIDTH), -6.0, -1.0)
    inp['ev_w2'] = nrm((N_EVEN, W_LORA, RWKV_WIDTH), 0.1 * W_LORA ** -0.5)
    inp['ev_a0'] = nrm((N_EVEN, RWKV_WIDTH), 0.1)
    inp['ev_a2'] = nrm((N_EVEN, A_LORA, RWKV_WIDTH), 0.1 * A_LORA ** -0.5)
    inp['ev_g2'] = nrm((N_EVEN, G_LORA, RWKV_WIDTH), G_LORA ** -0.5)
    inp['ev_k_k'] = 0.85 + nrm((N_EVEN, RWKV_WIDTH), 0.05)
    inp['ev_k_a'] = 1.0 + nrm((N_EVEN, RWKV_WIDTH), 0.05)
    inp['ev_r_k'] = nrm((N_EVEN, RWKV_HEADS, RWKV_HEAD_DIM), 0.1)
    inp['ev_gn_g'] = 1.0 + nrm((N_EVEN, RWKV_WIDTH), 0.05)
    inp['ev_gn_b'] = nrm((N_EVEN, RWKV_WIDTH), 0.01)
    inp['ev_lru_conv_w'] = nrm((N_EVEN, LRU_CONV, LRU_WIDTH), LRU_CONV ** -0.5)
    inp['ev_lru_conv_b'] = nrm((N_EVEN, LRU_WIDTH), 0.01)
    inp['ev_lru_wa'] = nrm((N_EVEN, LRU_BLOCKS, LRU_BLOCK, LRU_BLOCK), LRU_BLOCK ** -0.5)
    inp['ev_lru_ba'] = nrm((N_EVEN, LRU_WIDTH), 0.01)
    inp['ev_lru_wx'] = nrm((N_EVEN, LRU_BLOCKS, LRU_BLOCK, LRU_BLOCK), LRU_BLOCK ** -0.5)
    inp['ev_lru_bx'] = nrm((N_EVEN, LRU_WIDTH), 0.01)
    a_c = unif((N_EVEN, LRU_WIDTH), 0.9, 0.999) ** (1.0 / LRU_C)
    inp['ev_lru_lam'] = jnp.log(a_c) - jnp.log1p(-a_c)
    inp['ev_w_out'] = nrm((N_EVEN, EVEN_MIX, D_MODEL), DN_BETA * EVEN_MIX ** -0.5)
    inp['od_w_in'] = nrm((N_ODD, D_MODEL, ODD_IN), d_in)
    inp['od_cf_conv_w'] = nrm((N_ODD, CONF_CONV, CONF_WIDTH), CONF_CONV ** -0.5)
    inp['od_cf_conv_b'] = nrm((N_ODD, CONF_WIDTH), 0.01)
    inp['od_cf_ln_g'] = 1.0 + nrm((N_ODD, CONF_WIDTH), 0.05)
    inp['od_cf_ln_b'] = nrm((N_ODD, CONF_WIDTH), 0.01)
    inp['od_ssm_conv_w'] = nrm((N_ODD, SSM_CONV, SSM_XBC), SSM_CONV ** -0.5)
    inp['od_ssm_conv_b'] = nrm((N_ODD, SSM_XBC), 0.01)
    dt0 = jnp.maximum(jnp.exp(unif((N_ODD, SSM_HEADS), math.log(1e-3), math.log(1e-1))), 1e-4)
    inp['od_dt_bias'] = dt0 + jnp.log(-jnp.expm1(-dt0))
    inp['od_a_log'] = jnp.log(unif((N_ODD, SSM_HEADS), 1.0, 16.0))
    inp['od_d_skip'] = 1.0 + nrm((N_ODD, SSM_HEADS), 0.1)
    inp['od_ssm_norm_g'] = 1.0 + nrm((N_ODD, SSM_INNER), 0.05)
    inp['od_w_out'] = nrm((N_ODD, ODD_MIX, D_MODEL), DN_BETA * ODD_MIX ** -0.5)
    inp['xa_wq'] = nrm((DEPTH, D_MODEL, D_MODEL), d_in)
    inp['xa_wk'] = nrm((DEPTH, D_MODEL, D_MODEL), d_in)
    inp['xa_wv'] = nrm((DEPTH, D_MODEL, D_MODEL), d_in)
    inp['xa_wo'] = nrm((DEPTH, D_MODEL, D_MODEL), DN_BETA * d_in)
    inp['ffn_w1'] = nrm((DEPTH, D_MODEL, D_FF), d_in)
    inp['ffn_w2'] = nrm((DEPTH, D_FF, D_MODEL), DN_BETA * D_FF ** -0.5)
    inp['ln_mix_g'] = 1.0 + nrm((DEPTH, D_MODEL), 0.05)
    inp['ln_mix_b'] = nrm((DEPTH, D_MODEL), 0.01)
    inp['ln_mem_g'] = 1.0 + nrm((DEPTH, D_MODEL), 0.05)
    inp['ln_mem_b'] = nrm((DEPTH, D_MODEL), 0.01)
    inp['ln_ffn_g'] = 1.0 + nrm((DEPTH, D_MODEL), 0.05)
    inp['ln_ffn_b'] = nrm((DEPTH, D_MODEL), 0.01)
    return inp


def reference(x, mem, ev_w_in, ev_mu, ev_w0, ev_w2, ev_a0, ev_a2, ev_g2, ev_k_k, ev_k_a, ev_r_k,
              ev_gn_g, ev_gn_b, ev_lru_conv_w, ev_lru_conv_b, ev_lru_wa, ev_lru_ba, ev_lru_wx, ev_lru_bx,
              ev_lru_lam, ev_w_out, od_w_in, od_cf_conv_w, od_cf_conv_b, od_cf_ln_g, od_cf_ln_b,
              od_ssm_conv_w, od_ssm_conv_b, od_dt_bias, od_a_log, od_d_skip, od_ssm_norm_g, od_w_out,
              xa_wq, xa_wk, xa_wv, xa_wo, ffn_w1, ffn_w2, ln_mix_g, ln_mix_b, ln_mem_g, ln_mem_b,
              ln_ffn_g, ln_ffn_b):
    h = x
    for i in range(DEPTH):
        j = i // 2
        if i % 2 == 0:
            p = h @ ev_w_in[j]
            y_a = _rwkv7_time_mix(p[..., :RWKV_COLS], ev_mu[j], ev_w0[j], ev_w2[j], ev_a0[j], ev_a2[j],
                                  ev_g2[j], ev_k_k[j], ev_k_a[j], ev_r_k[j], ev_gn_g[j], ev_gn_b[j])
            y_b = _rglru_block(p[..., RWKV_COLS:RWKV_COLS + LRU_WIDTH], p[..., RWKV_COLS + LRU_WIDTH:],
                               ev_lru_conv_w[j], ev_lru_conv_b[j], ev_lru_wa[j], ev_lru_ba[j],
                               ev_lru_wx[j], ev_lru_bx[j], ev_lru_lam[j])
            mix = jnp.concatenate([y_a, y_b], axis=-1).astype(h.dtype) @ ev_w_out[j]
        else:
            p = h @ od_w_in[j]
            o1 = 2 * CONF_WIDTH
            o2 = o1 + SSM_INNER
            o3 = o2 + SSM_XBC
            y_c = _conformer_conv(p[..., :o1], od_cf_conv_w[j], od_cf_conv_b[j], od_cf_ln_g[j], od_cf_ln_b[j])
            y_d = _mamba2_ssd(p[..., o1:o2], p[..., o2:o3], p[..., o3:], od_ssm_conv_w[j], od_ssm_conv_b[j],
                              od_dt_bias[j], od_a_log[j], od_d_skip[j], od_ssm_norm_g[j])
            mix = jnp.concatenate([y_c, y_d], axis=-1).astype(h.dtype) @ od_w_out[j]
        h = _layer_norm(DN_ALPHA * h + mix, ln_mix_g[i], ln_mix_b[i])
        h = _layer_norm(DN_ALPHA * h + _mem_cross_attn(h, mem, xa_wq[i], xa_wk[i], xa_wv[i], xa_wo[i]),
                        ln_mem_g[i], ln_mem_b[i])
        ffn = jnp.square(jax.nn.relu(h @ ffn_w1[i])) @ ffn_w2[i]
        h = _layer_norm(DN_ALPHA * h + ffn, ln_ffn_g[i], ln_ffn_b[i])
    return h
```

```python
import functools

import jax
import jax.numpy as jnp
import numpy as np
from jax import lax
from jax.experimental import pallas as pl
from jax.experimental.pallas import tpu as pltpu

F32 = jnp.float32
BF16 = jnp.bfloat16
HI = lax.Precision.HIGHEST

D_MODEL = 1024
DEPTH = 2
DN_ALPHA = (2 * DEPTH) ** 0.25
LN_EPS = 1e-5

RWKV_WIDTH = 512
RWKV_HEAD_DIM = 64
RWKV_HEADS = 8
RWKV_GN_EPS = 64e-5
RWKV_CHUNK = 64

LRU_WIDTH = 512
LRU_C = 8.0
LRU_CONV = 4
LRU_TILE = 256

CONF_WIDTH = 512
CONF_CONV = 31
CONF_HALO = 32
CONF_TILE = 256

SSM_INNER = 1024
SSM_HEAD_DIM = 64
SSM_HEADS = 16
SSM_GROUPS = 2
SSM_STATE = 128
SSM_CONV = 4
SSM_CHUNK = 128
ODD_IN = 3600
ODD_IN_PAD = 3712

XA_HEADS = 4
XA_HEAD_DIM = 256
D_FF = 4096
FF_CHUNK = 1024

SUBLANES = 8
LANES = 128
VMEM_CAP = 56 * 1024 * 1024


def _cparams(semantics, vmem_bytes):
    return pltpu.CompilerParams(dimension_semantics=semantics,
                                vmem_limit_bytes=int(min(max(vmem_bytes, 16 * 1024 * 1024), VMEM_CAP)))


def _dot(a, b):
    return jnp.dot(a.astype(BF16), b.astype(BF16), preferred_element_type=F32)


def _dot_hi(a, b):
    return jnp.dot(a, b, preferred_element_type=F32, precision=HI)


def _dot_nt_hi(a, b):
    return lax.dot_general(a, b, (((1,), (1,)), ((), ())), preferred_element_type=F32, precision=HI)


def _dot_nt(a, b):
    return lax.dot_general(a.astype(BF16), b.astype(BF16), (((1,), (1,)), ((), ())),
                           preferred_element_type=F32)


def _sigmoid(x):
    return 1.0 / (1.0 + jnp.exp(-x))


def _softplus(x):
    return jnp.maximum(x, 0.0) + jnp.log(1.0 + jnp.exp(-jnp.abs(x)))


def _silu(x):
    return x * _sigmoid(x)


def _gelu_tanh(x):
    return 0.5 * x * (1.0 + jnp.tanh(0.7978845608028654 * (x + 0.044715 * (x * x * x))))


def _res_ln(h, y, g, b):
    z = DN_ALPHA * h + y
    mu = jnp.mean(z, axis=-1, keepdims=True)
    zc = z - mu
    var = jnp.mean(zc * zc, axis=-1, keepdims=True)
    return zc * lax.rsqrt(var + LN_EPS) * g + b


def _proj_kernel(x_ref, w_ref, o_ref):
    o_ref[...] = _dot(x_ref[...], w_ref[...]).astype(o_ref.dtype)


def _proj(x, w, tm, out_dtype=F32):
    n, k = x.shape
    m = w.shape[1]
    vmem = 2 * (tm * k * x.dtype.itemsize + k * m * 2 + tm * m * jnp.dtype(out_dtype).itemsize) + (4 << 20)
    return pl.pallas_call(
        _proj_kernel,
        grid=(n // tm,),
        in_specs=[pl.BlockSpec((tm, k), lambda i: (i, 0)), pl.BlockSpec((k, m), lambda i: (0, 0))],
        out_specs=pl.BlockSpec((tm, m), lambda i: (i, 0)),
        out_shape=jax.ShapeDtypeStruct((n, m), out_dtype),
        compiler_params=_cparams(("parallel",), vmem),
    )(x, w)


def _mix_out_kernel(a_ref, b_ref, wa_ref, wb_ref, h_ref, g_ref, bias_ref, o_ref):
    y = _dot(a_ref[...], wa_ref[...]) + _dot(b_ref[...], wb_ref[...])
    o_ref[...] = _res_ln(h_ref[...], y, g_ref[...], bias_ref[...])


def _mix_out(a, b, wa, wb, h, g, bias, tm):
    n = h.shape[0]
    ka, kb = a.shape[1], b.shape[1]
    vmem = 2 * (tm * (ka + kb + 2 * D_MODEL) * 4 + (ka + kb) * D_MODEL * 2) + (8 << 20)
    row = lambda i: (i, 0)
    fix = lambda i: (0, 0)
    return pl.pallas_call(
        _mix_out_kernel,
        grid=(n // tm,),
        in_specs=[pl.BlockSpec((tm, ka), row), pl.BlockSpec((tm, kb), row),
                  pl.BlockSpec((ka, D_MODEL), fix), pl.BlockSpec((kb, D_MODEL), fix),
                  pl.BlockSpec((tm, D_MODEL), row), pl.BlockSpec((1, D_MODEL), fix),
                  pl.BlockSpec((1, D_MODEL), fix)],
        out_specs=pl.BlockSpec((tm, D_MODEL), row),
        out_shape=jax.ShapeDtypeStruct((n, D_MODEL), F32),
        compiler_params=_cparams(("parallel",), vmem),
    )(a, b, wa, wb, h, g, bias)


def _attn_kernel(h_ref, k_ref, v_ref, wq_ref, wo_ref, g_ref, b_ref, o_ref, cat_ref):
    h = h_ref[...]
    q = _dot(h, wq_ref[...])
    for hd in range(XA_HEADS):
        sl = slice(hd * XA_HEAD_DIM, (hd + 1) * XA_HEAD_DIM)
        s = _dot_nt(q[:, sl], k_ref[:, sl]) * (XA_HEAD_DIM ** -0.5)
        s = s - jnp.max(s, axis=-1, keepdims=True)
        e = jnp.exp(s)
        p = e / jnp.sum(e, axis=-1, keepdims=True)
        cat_ref[:, sl] = _dot(p, v_ref[:, sl])
    y = _dot(cat_ref[...], wo_ref[...])
    o_ref[...] = _res_ln(h, y, g_ref[...], b_ref[...])


def _attn(h, kmem, vmem_, wq, wo, g, b, bsz, tm):
    n = h.shape[0]
    tiles = n // bsz // tm
    n_mem = kmem.shape[0] // bsz
    vmem = 2 * (2 * tm * D_MODEL * 4 + 2 * n_mem * D_MODEL * 2 + 2 * D_MODEL * D_MODEL * 2) + tm * D_MODEL * 16 + (8 << 20)
    row = lambda bi, i: (bi * tiles + i, 0)
    mem = lambda bi, i: (bi, 0)
    fix = lambda bi, i: (0, 0)
    return pl.pallas_call(
        _attn_kernel,
        grid=(bsz, tiles),
        in_specs=[pl.BlockSpec((tm, D_MODEL), row), pl.BlockSpec((n_mem, D_MODEL), mem),
                  pl.BlockSpec((n_mem, D_MODEL), mem), pl.BlockSpec((D_MODEL, D_MODEL), fix),
                  pl.BlockSpec((D_MODEL, D_MODEL), fix), pl.BlockSpec((1, D_MODEL), fix),
                  pl.BlockSpec((1, D_MODEL), fix)],
        out_specs=pl.BlockSpec((tm, D_MODEL), row),
        out_shape=jax.ShapeDtypeStruct((n, D_MODEL), F32),
        scratch_shapes=[pltpu.VMEM((tm, D_MODEL), F32)],
        compiler_params=_cparams(("parallel", "parallel"), vmem),
    )(h, kmem, vmem_, wq, wo, g, b)


def _ffn_kernel(h_ref, w1_ref, w2_ref, g_ref, b_ref, o_ref, acc_ref):
    c = pl.program_id(1)
    h = h_ref[...]
    u = jnp.square(jnp.maximum(_dot(h, w1_ref[...]), 0.0))
    part = _dot(u, w2_ref[...])

    @pl.when(c == 0)
    def _():
        acc_ref[...] = part

    @pl.when(c > 0)
    def _():
        acc_ref[...] += part

    @pl.when(c == pl.num_programs(1) - 1)
    def _():
        o_ref[...] = _res_ln(h, acc_ref[...], g_ref[...], b_ref[...])


def _ffn(h, w1, w2, g, b, tm):
    n = h.shape[0]
    vmem = 2 * (2 * tm * D_MODEL * 4 + 2 * D_MODEL * FF_CHUNK * 2) + tm * D_MODEL * 4 + tm * FF_CHUNK * 8 + (8 << 20)
    return pl.pallas_call(
        _ffn_kernel,
        grid=(n // tm, D_FF // FF_CHUNK),
        in_specs=[pl.BlockSpec((tm, D_MODEL), lambda i, c: (i, 0)),
                  pl.BlockSpec((D_MODEL, FF_CHUNK), lambda i, c: (0, c)),
                  pl.BlockSpec((FF_CHUNK, D_MODEL), lambda i, c: (c, 0)),
                  pl.BlockSpec((1, D_MODEL), lambda i, c: (0, 0)),
                  pl.BlockSpec((1, D_MODEL), lambda i, c: (0, 0))],
        out_specs=pl.BlockSpec((tm, D_MODEL), lambda i, c: (i, 0)),
        out_shape=jax.ShapeDtypeStruct((n, D_MODEL), F32),
        scratch_shapes=[pltpu.VMEM((tm, D_MODEL), F32)],
        compiler_params=_cparams(("parallel", "arbitrary"), vmem),
    )(h, w1, w2, g, b)


def _shifted(ext_ref, x, first, rows):
    @pl.when(first)
    def _():
        ext_ref[0:SUBLANES, :] = jnp.zeros((SUBLANES, x.shape[1]), F32)

    ext_ref[SUBLANES:SUBLANES + rows, :] = x
    prev = ext_ref[SUBLANES - 1:SUBLANES - 1 + rows, :]
    ext_ref[0:SUBLANES, :] = x[rows - SUBLANES:rows, :]
    return prev


def _rwkv_kernel(rkv_ref, wa_ref, gd_ref, mu_rkv_ref, mu_wa_ref, mu_gd_ref, w0_ref, w2_ref, a0_ref, a2_ref,
                 g2_ref, kk_ref, ka_ref, rk_ref, gng_ref, gnb_ref, hsum_ref, tril_ref, o_ref,
                 ext_rkv, ext_wa, ext_gd, state_ref, obuf_ref):
    L = RWKV_CHUNK
    N = RWKV_HEAD_DIM
    first = pl.program_id(1) == 0

    @pl.when(first)
    def _():
        state_ref[...] = jnp.zeros(state_ref.shape, F32)

    x = rkv_ref[...]
    xm = x + (_shifted(ext_rkv, x, first, L) - x) * mu_rkv_ref[...]
    wa = wa_ref[...]
    wam = wa + (_shifted(ext_wa, wa, first, L) - wa) * mu_wa_ref[...]
    gd = gd_ref[...]
    gdm = gd + (_shifted(ext_gd, gd, first, L) - gd) * mu_gd_ref[...]

    r = xm[:, 0:RWKV_WIDTH]
    k = xm[:, RWKV_WIDTH:2 * RWKV_WIDTH]
    v = xm[:, 2 * RWKV_WIDTH:3 * RWKV_WIDTH]
    w = -_softplus(-(w0_ref[...] + _dot_hi(jnp.tanh(wam), w2_ref[...]))) - 0.5
    logdec = -jnp.exp(w)
    a = _sigmoid(a0_ref[...] + _dot_hi(wam, a2_ref[...]))
    g = _dot_hi(_sigmoid(gdm), g2_ref[...])
    hsum = hsum_ref[...]
    kks = k * kk_ref[...]
    kk = kks / jnp.maximum(jnp.sqrt(_dot_hi(kks * kks, hsum)), 1e-12)
    k2 = k * (1.0 + (a - 1.0) * ka_ref[...])
    bonus = _dot_hi(r * k2 * rk_ref[...], hsum) * v
    beta = kk * a

    cum = _dot_hi(tril_ref[...], logdec)
    cum_l = cum[L - 1:L, :]
    e_to_end = jnp.exp(cum_l - cum)
    e_neg = jnp.exp(-cum)
    a_t = -kk * jnp.exp(cum - logdec)
    r_t = r * jnp.exp(cum)
    b_t = beta * e_neg
    k_t = k2 * e_neg
    b_p = beta * e_to_end
    k_p = k2 * e_to_end
    p_l = jnp.exp(cum_l)

    row = lax.broadcasted_iota(jnp.int32, (L, L), 0)
    col = lax.broadcasted_iota(jnp.int32, (L, L), 1)
    strict = col < row
    incl = col <= row
    eye = (lax.broadcasted_iota(jnp.int32, (N, N), 0) == lax.broadcasted_iota(jnp.int32, (N, N), 1)).astype(F32)

    for h in range(RWKV_HEADS):
        sl = slice(h * N, (h + 1) * N)
        ah, rh, bh, kh, vh = a_t[:, sl], r_t[:, sl], b_t[:, sl], k_t[:, sl], v[:, sl]
        n_ab = jnp.where(strict, _dot_nt_hi(ah, bh), 0.0)
        n_ak = jnp.where(strict, _dot_nt_hi(ah, kh), 0.0)
        a_rb = jnp.where(incl, _dot_nt_hi(rh, bh), 0.0)
        a_rk = jnp.where(incl, _dot_nt_hi(rh, kh), 0.0)
        xs = jnp.concatenate([ah, _dot_hi(n_ak, vh)], axis=1)
        m = n_ab
        step = 1
        while step < L:
            xs = xs + _dot_hi(m, xs)
            step *= 2
            if step < L:
                m = _dot_hi(m, m)
        ta = xs[:, 0:N]
        w2 = xs[:, N:2 * N]
        s0 = state_ref[h]
        arx = _dot_hi(a_rb, xs)
        q_t = rh + arx[:, 0:N]
        o_loc = arx[:, N:2 * N] + _dot_hi(a_rk, vh)
        o_h = _dot_nt_hi(q_t, s0) + o_loc
        xb = _dot_hi(xs.T, b_p[:, sl])
        g_mat = eye * p_l[:, sl] + xb[0:N, :]
        s_loc = xb[N:2 * N, :] + _dot_hi(vh.T, k_p[:, sl])
        state_ref[h] = _dot_hi(s0, g_mat) + s_loc
        mean = jnp.mean(o_h, axis=-1, keepdims=True)
        oc = o_h - mean
        var = jnp.mean(oc * oc, axis=-1, keepdims=True)
        obuf_ref[:, sl] = oc * lax.rsqrt(var + RWKV_GN_EPS)

    o_ref[...] = (obuf_ref[...] * gng_ref[...] + gnb_ref[...] + bonus) * g


def _rwkv(p, prm, bsz, seq):
    L = RWKV_CHUNK
    tiles = seq // L
    n = bsz * seq
    row = lambda c: (lambda bi, i: (bi * tiles + i, c))
    fix = lambda bi, i: (0, 0)
    vec = lambda width: pl.BlockSpec((1, width), fix)
    in_specs = [pl.BlockSpec((L, 3 * RWKV_WIDTH), row(0)),
                pl.BlockSpec((L, LANES), row(20)), pl.BlockSpec((L, LANES), row(21)),
                vec(3 * RWKV_WIDTH), vec(LANES), vec(LANES),
                vec(RWKV_WIDTH), pl.BlockSpec((LANES, RWKV_WIDTH), fix),
                vec(RWKV_WIDTH), pl.BlockSpec((LANES, RWKV_WIDTH), fix),
                pl.BlockSpec((LANES, RWKV_WIDTH), fix),
                vec(RWKV_WIDTH), vec(RWKV_WIDTH), vec(RWKV_WIDTH), vec(RWKV_WIDTH), vec(RWKV_WIDTH),
                pl.BlockSpec((RWKV_WIDTH, RWKV_WIDTH), fix), pl.BlockSpec((L, L), fix)]
    return pl.pallas_call(
        _rwkv_kernel,
        grid=(bsz, tiles),
        in_specs=in_specs,
        out_specs=pl.BlockSpec((L, RWKV_WIDTH), lambda bi, i: (bi * tiles + i, 0)),
        out_shape=jax.ShapeDtypeStruct((n, RWKV_WIDTH), F32),
        scratch_shapes=[pltpu.VMEM((SUBLANES + L, 3 * RWKV_WIDTH), F32),
                        pltpu.VMEM((SUBLANES + L, LANES), F32),
                        pltpu.VMEM((SUBLANES + L, LANES), F32),
                        pltpu.VMEM((RWKV_HEADS, RWKV_HEAD_DIM, RWKV_HEAD_DIM), F32),
                        pltpu.VMEM((L, RWKV_WIDTH), F32)],
        compiler_params=_cparams(("arbitrary", "arbitrary"), 32 << 20),
    )(p, p, p, *prm)


def _causal_conv(ext_ref, x, w_ref, b_ref, first, rows, taps, halo):
    @pl.when(first)
    def _():
        ext_ref[0:halo, :] = jnp.zeros((halo, x.shape[1]), F32)

    ext_ref[halo:halo + rows, :] = x
    acc = x * w_ref[taps - 1:taps, :] + b_ref[...]
    for j in range(1, taps):
        acc = acc + ext_ref[halo - j:halo - j + rows, :] * w_ref[taps - 1 - j:taps - j, :]
    ext_ref[0:halo, :] = x[rows - halo:rows, :]
    return acc


def _lru_kernel(xb_ref, gb_ref, cw_ref, cb_ref, wa_ref, ba_ref, wx_ref, bx_ref, lam_ref, o_ref,
                ext_ref, carry_ref):
    rows = LRU_TILE
    first = pl.program_id(1) == 0

    @pl.when(first)
    def _():
        carry_ref[...] = jnp.zeros(carry_ref.shape, F32)

    xc = _causal_conv(ext_ref, xb_ref[...], cw_ref, cb_ref, first, rows, LRU_CONV, SUBLANES)
    gate_r = _sigmoid(_dot(xc, wa_ref[...]) + ba_ref[...])
    gate_i = _sigmoid(_dot(xc, wx_ref[...]) + bx_ref[...])
    log_a = -LRU_C * _softplus(-lam_ref[...]) * gate_r
    a = jnp.exp(log_a)
    u = jnp.sqrt(1.0 - jnp.exp(2.0 * log_a)) * (gate_i * xc)
    ridx = lax.broadcasted_iota(jnp.int32, (rows, LRU_WIDTH), 0)
    s = 1
    while s < rows:
        keep = ridx >= s
        a_sh = jnp.where(keep, pltpu.roll(a, s, axis=0), 1.0)
        u_sh = jnp.where(keep, pltpu.roll(u, s, axis=0), 0.0)
        u = a * u_sh + u
        a = a * a_sh
        s *= 2
    h = a * carry_ref[0:1, :] + u
    carry_ref[...] = jnp.broadcast_to(h[rows - 1:rows, :], carry_ref.shape)
    o_ref[...] = h * _gelu_tanh(gb_ref[...])


def _lru(p, prm, bsz, seq):
    rows = LRU_TILE
    tiles = seq // rows
    n = bsz * seq
    row = lambda c: (lambda bi, i: (bi * tiles + i, c))
    fix = lambda bi, i: (0, 0)
    vec = pl.BlockSpec((1, LRU_WIDTH), fix)
    mat = pl.BlockSpec((LRU_WIDTH, LRU_WIDTH), fix)
    return pl.pallas_call(
        _lru_kernel,
        grid=(bsz, tiles),
        in_specs=[pl.BlockSpec((rows, LRU_WIDTH), row(3)), pl.BlockSpec((rows, LRU_WIDTH), row(4)),
                  pl.BlockSpec((LRU_CONV, LRU_WIDTH), fix), vec, mat, vec, mat, vec, vec],
        out_specs=pl.BlockSpec((rows, LRU_WIDTH), lambda bi, i: (bi * tiles + i, 0)),
        out_shape=jax.ShapeDtypeStruct((n, LRU_WIDTH), F32),
        scratch_shapes=[pltpu.VMEM((SUBLANES + rows, LRU_WIDTH), F32),
                        pltpu.VMEM((SUBLANES, LRU_WIDTH), F32)],
        compiler_params=_cparams(("arbitrary", "arbitrary"), 32 << 20),
    )(p, p, *prm)


def _conf_kernel(c_ref, cw_ref, cb_ref, g_ref, b_ref, o_ref, ext_ref):
    rows = CONF_TILE
    first = pl.program_id(1) == 0
    c = c_ref[...]
    glu = c[:, 0:CONF_WIDTH] * _sigmoid(c[:, CONF_WIDTH:2 * CONF_WIDTH])
    u = _causal_conv(ext_ref, glu, cw_ref, cb_ref, first, rows, CONF_CONV, CONF_HALO)
    mu = jnp.mean(u, axis=-1, keepdims=True)
    uc = u - mu
    var = jnp.mean(uc * uc, axis=-1, keepdims=True)
    o_ref[...] = _silu(uc * lax.rsqrt(var + LN_EPS) * g_ref[...] + b_ref[...])


def _conf(p, prm, bsz, seq):
    rows = CONF_TILE
    tiles = seq // rows
    n = bsz * seq
    fix = lambda bi, i: (0, 0)
    vec = pl.BlockSpec((1, CONF_WIDTH), fix)
    return pl.pallas_call(
        _conf_kernel,
        grid=(bsz, tiles),
        in_specs=[pl.BlockSpec((rows, 2 * CONF_WIDTH), lambda bi, i: (bi * tiles + i, 0)),
                  pl.BlockSpec((CONF_CONV, CONF_WIDTH), fix), vec, vec, vec],
        out_specs=pl.BlockSpec((rows, CONF_WIDTH), lambda bi, i: (bi * tiles + i, 0)),
        out_shape=jax.ShapeDtypeStruct((n, CONF_WIDTH), F32),
        scratch_shapes=[pltpu.VMEM((CONF_HALO + rows, CONF_WIDTH), F32)],
        compiler_params=_cparams(("arbitrary", "arbitrary"), 32 << 20),
    )(p, *prm)


def _ssd_kernel(z_ref, xs_ref, bc_ref, dt_ref, cwx_ref, cbx_ref, cwb_ref, cbb_ref, dtb_ref, alog_ref,
                dskip_ref, normg_ref, expand_ref, tril_ref, o_ref, ext_xs, ext_bc, state_ref, ybuf_ref):
    L = SSM_CHUNK
    P = SSM_HEAD_DIM
    NS = SSM_STATE
    HG = SSM_HEADS // SSM_GROUPS
    first = pl.program_id(1) == 0

    @pl.when(first)
    def _():
        state_ref[...] = jnp.zeros(state_ref.shape, F32)

    xs = _silu(_causal_conv(ext_xs, xs_ref[...], cwx_ref, cbx_ref, first, L, SSM_CONV, SUBLANES))
    bc = _silu(_causal_conv(ext_bc, bc_ref[...], cwb_ref, cbb_ref, first, L, SSM_CONV, SUBLANES))
    dt = _softplus(dt_ref[...] + dtb_ref[...])
    a_head = -jnp.exp(alog_ref[...])
    acum = _dot_hi(tril_ref[...], dt * a_head)
    acum_t = acum.T
    acum_l = acum[L - 1:L, :]
    xdt = xs * _dot_hi(dt, expand_ref[...])
    to_end = jnp.exp(acum_l - acum)
    from_start = jnp.exp(acum)
    chunk_decay = jnp.exp(acum_l)
    causal = lax.broadcasted_iota(jnp.int32, (L, L), 1) <= lax.broadcasted_iota(jnp.int32, (L, L), 0)

    for g in range(SSM_GROUPS):
        bm = bc[:, g * NS:(g + 1) * NS]
        cm = bc[:, (SSM_GROUPS + g) * NS:(SSM_GROUPS + g + 1) * NS]
        cb = _dot_nt(cm, bm)
        for hh in range(HG):
            h = g * HG + hh
            sl = slice(h * P, (h + 1) * P)
            seg = acum[:, h:h + 1] - acum_t[h:h + 1, :]
            decay = jnp.exp(jnp.where(causal, seg, -jnp.inf))
            xh = xdt[:, sl]
            prev = state_ref[h]
            y = _dot(cb * decay, xh) + _dot_nt(cm, prev) * from_start[:, h:h + 1]
            st = _dot((xh * to_end[:, h:h + 1]).T, bm)
            state_ref[h] = prev * chunk_decay[:, h:h + 1] + st
            ybuf_ref[:, sl] = y

    y = (ybuf_ref[...] + dskip_ref[...] * xs) * _silu(z_ref[...])
    half = SSM_INNER // SSM_GROUPS
    for g in range(SSM_GROUPS):
        yg = y[:, g * half:(g + 1) * half]
        ms = jnp.mean(yg * yg, axis=-1, keepdims=True)
        o_ref[:, g * half:(g + 1) * half] = yg * lax.rsqrt(ms + LN_EPS) * normg_ref[:, g * half:(g + 1) * half]


def _ssd(p, prm, bsz, seq):
    L = SSM_CHUNK
    tiles = seq // L
    n = bsz * seq
    row = lambda c: (lambda bi, i: (bi * tiles + i, c))
    fix = lambda bi, i: (0, 0)
    bcw = 2 * SSM_GROUPS * SSM_STATE
    vec = lambda width: pl.BlockSpec((1, width), fix)
    in_specs = [pl.BlockSpec((L, SSM_INNER), row(1)), pl.BlockSpec((L, SSM_INNER), row(2)),
                pl.BlockSpec((L, bcw), row(6)), pl.BlockSpec((L, LANES), row(28)),
                pl.BlockSpec((SSM_CONV, SSM_INNER), fix), vec(SSM_INNER),
                pl.BlockSpec((SSM_CONV, bcw), fix), vec(bcw),
                vec(LANES), vec(LANES), vec(SSM_INNER), vec(SSM_INNER),
                pl.BlockSpec((LANES, SSM_INNER), fix), pl.BlockSpec((L, L), fix)]
    return pl.pallas_call(
        _ssd_kernel,
        grid=(bsz, tiles),
        in_specs=in_specs,
        out_specs=pl.BlockSpec((L, SSM_INNER), lambda bi, i: (bi * tiles + i, 0)),
        out_shape=jax.ShapeDtypeStruct((n, SSM_INNER), F32),
        scratch_shapes=[pltpu.VMEM((SUBLANES + L, SSM_INNER), F32),
                        pltpu.VMEM((SUBLANES + L, bcw), F32),
                        pltpu.VMEM((SSM_HEADS, SSM_HEAD_DIM, SSM_STATE), F32),
                        pltpu.VMEM((L, SSM_INNER), F32)],
        compiler_params=_cparams(("arbitrary", "arbitrary"), 32 << 20),
    )(p, p, p, p, *prm)


def _row(v):
    return v.reshape(1, -1).astype(F32)


def _pad_rows(m, rows, offset):
    out = jnp.zeros((rows, m.shape[1]), F32)
    return out.at[offset:offset + m.shape[0]].set(m)


def _block_diag(w):
    nb, d, e = w.shape
    eye = jnp.eye(nb, dtype=w.dtype)
    return (eye[:, None, :, None] * w[:, :, None, :]).reshape(nb * d, nb * e)


def _tril_ones(n):
    return jnp.asarray(np.tril(np.ones((n, n), np.float32)))


def _head_sum_matrix():
    idx = np.arange(RWKV_WIDTH) // RWKV_HEAD_DIM
    return jnp.asarray((idx[:, None] == idx[None, :]).astype(np.float32))


def _head_expand_matrix():
    m = np.zeros((LANES, SSM_INNER), np.float32)
    for h in range(SSM_HEADS):
        m[h, h * SSM_HEAD_DIM:(h + 1) * SSM_HEAD_DIM] = 1.0
    return jnp.asarray(m)


def _sublayers(h, mem2, i, bsz, xa_wq, xa_wk, xa_wv, xa_wo, ffn_w1, ffn_w2, ln_mem_g, ln_mem_b, ln_ffn_g, ln_ffn_b):
    kmem = _proj(mem2, xa_wk[i].astype(BF16), mem2.shape[0] // bsz, BF16)
    vmem_ = _proj(mem2, xa_wv[i].astype(BF16), mem2.shape[0] // bsz, BF16)
    h = _attn(h, kmem, vmem_, xa_wq[i].astype(BF16), xa_wo[i].astype(BF16), _row(ln_mem_g[i]), _row(ln_mem_b[i]),
              bsz, 512)
    return _ffn(h, ffn_w1[i].astype(BF16), ffn_w2[i].astype(BF16), _row(ln_ffn_g[i]), _row(ln_ffn_b[i]), 512)


def kernel(x, mem, ev_w_in, ev_mu, ev_w0, ev_w2, ev_a0, ev_a2, ev_g2, ev_k_k, ev_k_a, ev_r_k, ev_gn_g, ev_gn_b, ev_lru_conv_w, ev_lru_conv_b, ev_lru_wa, ev_lru_ba, ev_lru_wx, ev_lru_bx, ev_lru_lam, ev_w_out, od_w_in, od_cf_conv_w, od_cf_conv_b, od_cf_ln_g, od_cf_ln_b, od_ssm_conv_w, od_ssm_conv_b, od_dt_bias, od_a_log, od_d_skip, od_ssm_norm_g, od_w_out, xa_wq, xa_wk, xa_wv, xa_wo, ffn_w1, ffn_w2, ln_mix_g, ln_mix_b, ln_mem_g, ln_mem_b, ln_ffn_g, ln_ffn_b):
    bsz, seq, _ = x.shape
    n = bsz * seq
    h = x.reshape(n, D_MODEL)
    mem2 = mem.reshape(bsz * mem.shape[1], D_MODEL)
    xa = (xa_wq, xa_wk, xa_wv, xa_wo, ffn_w1, ffn_w2, ln_mem_g, ln_mem_b, ln_ffn_g, ln_ffn_b)

    rw = 3 * RWKV_WIDTH
    lo = 2 * LANES
    w_in = jnp.concatenate([ev_w_in[0][:, 0:rw], ev_w_in[0][:, rw + lo:], ev_w_in[0][:, rw:rw + lo]], axis=1)
    p = _proj(h, w_in.astype(BF16), 512)
    mu = ev_mu[0]
    rwkv_prm = (_row(mu[0:rw]), _row(mu[rw:rw + LANES]), _row(mu[rw + LANES:rw + lo]),
                _row(ev_w0[0]), _pad_rows(ev_w2[0], LANES, 0), _row(ev_a0[0]), _pad_rows(ev_a2[0], LANES, 64),
                ev_g2[0].astype(F32), _row(ev_k_k[0]), _row(ev_k_a[0]), _row(ev_r_k[0]), _row(ev_gn_g[0]),
                _row(ev_gn_b[0]), _head_sum_matrix(), _tril_ones(RWKV_CHUNK))
    y_a = _rwkv(p, rwkv_prm, bsz, seq)
    lru_prm = (ev_lru_conv_w[0], _row(ev_lru_conv_b[0]), _block_diag(ev_lru_wa[0]).astype(BF16), _row(ev_lru_ba[0]),
               _block_diag(ev_lru_wx[0]).astype(BF16), _row(ev_lru_bx[0]), _row(ev_lru_lam[0]))
    y_b = _lru(p, lru_prm, bsz, seq)
    w_out = ev_w_out[0].astype(BF16)
    h = _mix_out(y_a, y_b, w_out[0:RWKV_WIDTH], w_out[RWKV_WIDTH:], h, _row(ln_mix_g[0]), _row(ln_mix_b[0]), 512)
    h = _sublayers(h, mem2, 0, bsz, *xa)

    w_in = jnp.pad(od_w_in[0], ((0, 0), (0, ODD_IN_PAD - ODD_IN))).astype(BF16)
    p = _proj(h, w_in, 256)
    conf_prm = (od_cf_conv_w[0], _row(od_cf_conv_b[0]), _row(od_cf_ln_g[0]), _row(od_cf_ln_b[0]))
    y_c = _conf(p, conf_prm, bsz, seq)
    cw, cb = od_ssm_conv_w[0], od_ssm_conv_b[0]
    pad16 = lambda v: jnp.pad(v, (0, LANES - SSM_HEADS)).reshape(1, LANES).astype(F32)
    ssd_prm = (cw[:, 0:SSM_INNER], _row(cb[0:SSM_INNER]), cw[:, SSM_INNER:], _row(cb[SSM_INNER:]),
               pad16(od_dt_bias[0]), pad16(od_a_log[0]), _row(jnp.repeat(od_d_skip[0], SSM_HEAD_DIM)),
               _row(od_ssm_norm_g[0]), _head_expand_matrix(), _tril_ones(SSM_CHUNK))
    y_d = _ssd(p, ssd_prm, bsz, seq)
    w_out = od_w_out[0].astype(BF16)
    h = _mix_out(y_c, y_d, w_out[0:CONF_WIDTH], w_out[CONF_WIDTH:], h, _row(ln_mix_g[1]), _row(ln_mix_b[1]), 512)
    h = _sublayers(h, mem2, 1, bsz, *xa)
    return h.reshape(bsz, seq, D_MODEL)
```

```python
import functools

import jax
import jax.numpy as jnp
import numpy as np
from jax import lax
from jax.experimental import pallas as pl
from jax.experimental.pallas import tpu as pltpu

F32 = jnp.float32
BF16 = jnp.bfloat16
HI = lax.Precision.HIGHEST

D_MODEL = 1024
DEPTH = 2
DN_ALPHA = (2 * DEPTH) ** 0.25
LN_EPS = 1e-5

RWKV_WIDTH = 512
RWKV_HEAD_DIM = 64
RWKV_HEADS = 8
RWKV_GN_EPS = 64e-5
RWKV_CHUNK = 64
RWKV_STEP_CHUNKS = 4
RWKV_PREC_SCORE = "bf"
RWKV_PREC_SOLVE = "bf"
RWKV_PREC_STATE = "bf"

LRU_WIDTH = 512
LRU_C = 8.0
LRU_CONV = 4
LRU_TILE = 256

CONF_WIDTH = 512
CONF_CONV = 31
CONF_HALO = 32
CONF_TILE = 256

SSM_INNER = 1024
SSM_HEAD_DIM = 64
SSM_HEADS = 16
SSM_GROUPS = 2
SSM_STATE = 128
SSM_CONV = 4
SSM_CHUNK = 128
SSM_HEAD_BATCH = 4
ODD_IN = 3600
ODD_IN_PAD = 3712

XA_HEADS = 4
XA_HEAD_DIM = 256
D_FF = 4096

SUBLANES = 8
LANES = 128
VMEM_CAP = 56 * 1024 * 1024


def _cparams(semantics, vmem_bytes):
    return pltpu.CompilerParams(dimension_semantics=semantics,
                                vmem_limit_bytes=int(min(max(vmem_bytes, 16 * 1024 * 1024), VMEM_CAP)))


def _dot(a, b):
    return jnp.dot(a.astype(BF16), b.astype(BF16), preferred_element_type=F32)


def _split_parts(x, parts):
    out = []
    for _ in range(parts - 1):
        hi = x.astype(BF16)
        out.append(hi)
        x = x - hi.astype(F32)
    out.append(x.astype(BF16))
    return out


def _dot_onehot_rhs(x, m, parts):
    acc = None
    for xp in _split_parts(x, parts):
        t = jnp.dot(xp, m, preferred_element_type=F32)
        acc = t if acc is None else acc + t
    return acc


def _dot_onehot_lhs(m, x, parts):
    acc = None
    for xp in _split_parts(x, parts):
        t = jnp.dot(m, xp, preferred_element_type=F32)
        acc = t if acc is None else acc + t
    return acc


def _dot_nt(a, b):
    return lax.dot_general(a.astype(BF16), b.astype(BF16), (((1,), (1,)), ((), ())),
                           preferred_element_type=F32)


def _split_bf16(x):
    hi = x.astype(BF16)
    return hi, (x - hi.astype(F32)).astype(BF16)


def _mm_dims(a, b, dims, mode):
    if mode == "hi":
        return lax.dot_general(a, b, dims, preferred_element_type=F32, precision=HI)
    if mode == "bf":
        return lax.dot_general(a.astype(BF16), b.astype(BF16), dims, preferred_element_type=F32)
    a_hi, a_lo = _split_bf16(a)
    b_hi, b_lo = _split_bf16(b)
    d = lambda u, w: lax.dot_general(u, w, dims, preferred_element_type=F32)
    return d(a_hi, b_hi) + (d(a_hi, b_lo) + d(a_lo, b_hi))


def _mm(a, b, mode):
    return _mm_dims(a, b, (((1,), (0,)), ((), ())), mode)


def _mm_nt(a, b, mode):
    return _mm_dims(a, b, (((1,), (1,)), ((), ())), mode)


def _sigmoid(x):
    return 1.0 / (1.0 + jnp.exp(-x))


def _softplus(x):
    return jnp.maximum(x, 0.0) + jnp.log(1.0 + jnp.exp(-jnp.abs(x)))


def _silu(x):
    return x * _sigmoid(x)


def _gelu_tanh(x):
    return 0.5 * x * (1.0 + jnp.tanh(0.7978845608028654 * (x + 0.044715 * (x * x * x))))


def _res_ln(h, y, g, b):
    z = DN_ALPHA * h + y
    mu = jnp.mean(z, axis=-1, keepdims=True)
    zc = z - mu
    var = jnp.mean(zc * zc, axis=-1, keepdims=True)
    return zc * lax.rsqrt(var + LN_EPS) * g + b


def _proj_kernel(x_ref, w_ref, o_ref):
    o_ref[...] = _dot(x_ref[...], w_ref[...]).astype(o_ref.dtype)


def _proj(x, w, tm, out_dtype=F32):
    n, k = x.shape
    m = w.shape[1]
    vmem = 2 * (tm * k * x.dtype.itemsize + k * m * 2 + tm * m * jnp.dtype(out_dtype).itemsize) + (4 << 20)
    return pl.pallas_call(
        _proj_kernel,
        grid=(n // tm,),
        in_specs=[pl.BlockSpec((tm, k), lambda i: (i, 0)), pl.BlockSpec((k, m), lambda i: (0, 0))],
        out_specs=pl.BlockSpec((tm, m), lambda i: (i, 0)),
        out_shape=jax.ShapeDtypeStruct((n, m), out_dtype),
        compiler_params=_cparams(("parallel",), vmem),
    )(x, w)


def _mix_out_kernel(a_ref, b_ref, wa_ref, wb_ref, h_ref, g_ref, bias_ref, o_ref):
    y = _dot(a_ref[...], wa_ref[...]) + _dot(b_ref[...], wb_ref[...])
    o_ref[...] = _res_ln(h_ref[...], y, g_ref[...], bias_ref[...])


def _mix_out(a, b, wa, wb, h, g, bias, tm):
    n = h.shape[0]
    ka, kb = a.shape[1], b.shape[1]
    vmem = 2 * (tm * (ka + kb + 2 * D_MODEL) * 4 + (ka + kb) * D_MODEL * 2) + (8 << 20)
    row = lambda i: (i, 0)
    fix = lambda i: (0, 0)
    return pl.pallas_call(
        _mix_out_kernel,
        grid=(n // tm,),
        in_specs=[pl.BlockSpec((tm, ka), row), pl.BlockSpec((tm, kb), row),
                  pl.BlockSpec((ka, D_MODEL), fix), pl.BlockSpec((kb, D_MODEL), fix),
                  pl.BlockSpec((tm, D_MODEL), row), pl.BlockSpec((1, D_MODEL), fix),
                  pl.BlockSpec((1, D_MODEL), fix)],
        out_specs=pl.BlockSpec((tm, D_MODEL), row),
        out_shape=jax.ShapeDtypeStruct((n, D_MODEL), F32),
        compiler_params=_cparams(("parallel",), vmem),
    )(a, b, wa, wb, h, g, bias)


def _attn_kernel(h_ref, k_ref, v_ref, wq_ref, wo_ref, g_ref, b_ref, o_ref, cat_ref):
    h = h_ref[...]
    q = _dot(h, wq_ref[...])
    for hd in range(XA_HEADS):
        sl = slice(hd * XA_HEAD_DIM, (hd + 1) * XA_HEAD_DIM)
        s = _dot_nt(q[:, sl], k_ref[:, sl]) * (XA_HEAD_DIM ** -0.5)
        s = s - jnp.max(s, axis=-1, keepdims=True)
        e = jnp.exp(s)
        p = e / jnp.sum(e, axis=-1, keepdims=True)
        cat_ref[:, sl] = _dot(p, v_ref[:, sl])
    y = _dot(cat_ref[...], wo_ref[...])
    o_ref[...] = _res_ln(h, y, g_ref[...], b_ref[...])


def _attn(h, kmem, vmem_, wq, wo, g, b, bsz, tm):
    n = h.shape[0]
    tiles = n // bsz // tm
    n_mem = kmem.shape[0] // bsz
    vmem = 2 * (2 * tm * D_MODEL * 4 + 2 * n_mem * D_MODEL * 2 + 2 * D_MODEL * D_MODEL * 2) + tm * D_MODEL * 16 + (8 << 20)
    row = lambda bi, i: (bi * tiles + i, 0)
    mem = lambda bi, i: (bi, 0)
    fix = lambda bi, i: (0, 0)
    return pl.pallas_call(
        _attn_kernel,
        grid=(bsz, tiles),
        in_specs=[pl.BlockSpec((tm, D_MODEL), row), pl.BlockSpec((n_mem, D_MODEL), mem),
                  pl.BlockSpec((n_mem, D_MODEL), mem), pl.BlockSpec((D_MODEL, D_MODEL), fix),
                  pl.BlockSpec((D_MODEL, D_MODEL), fix), pl.BlockSpec((1, D_MODEL), fix),
                  pl.BlockSpec((1, D_MODEL), fix)],
        out_specs=pl.BlockSpec((tm, D_MODEL), row),
        out_shape=jax.ShapeDtypeStruct((n, D_MODEL), F32),
        scratch_shapes=[pltpu.VMEM((tm, D_MODEL), F32)],
        compiler_params=_cparams(("parallel", "parallel"), vmem),
    )(h, kmem, vmem_, wq, wo, g, b)


def _ffn_kernel(h_ref, w1_ref, w2_ref, g_ref, b_ref, o_ref):
    h = h_ref[...]
    u = jnp.square(jnp.maximum(_dot(h, w1_ref[...]), 0.0))
    o_ref[...] = _res_ln(h, _dot(u, w2_ref[...]), g_ref[...], b_ref[...])


def _ffn(h, w1, w2, g, b, tm):
    n = h.shape[0]
    vmem = 2 * D_MODEL * D_FF * 2 + 4 * tm * D_MODEL * 4 + tm * D_FF * 6 + tm * D_MODEL * 6 + (6 << 20)
    fix = lambda i: (0, 0)
    return pl.pallas_call(
        _ffn_kernel,
        grid=(n // tm,),
        in_specs=[pl.BlockSpec((tm, D_MODEL), lambda i: (i, 0)),
                  pl.BlockSpec((D_MODEL, D_FF), fix, pipeline_mode=pl.Buffered(1)),
                  pl.BlockSpec((D_FF, D_MODEL), fix, pipeline_mode=pl.Buffered(1)),
                  pl.BlockSpec((1, D_MODEL), fix),
                  pl.BlockSpec((1, D_MODEL), fix)],
        out_specs=pl.BlockSpec((tm, D_MODEL), lambda i: (i, 0)),
        out_shape=jax.ShapeDtypeStruct((n, D_MODEL), F32),
        compiler_params=_cparams(("parallel",), vmem),
    )(h, w1, w2, g, b)


def _shifted(ext_ref, x, first, rows):
    @pl.when(first)
    def _():
        ext_ref[0:SUBLANES, :] = jnp.zeros((SUBLANES, x.shape[1]), F32)

    ext_ref[SUBLANES:SUBLANES + rows, :] = x
    prev = ext_ref[SUBLANES - 1:SUBLANES - 1 + rows, :]
    ext_ref[0:SUBLANES, :] = x[rows - SUBLANES:rows, :]
    return prev


def _rwkv_kernel(rkv_ref, wa_ref, gd_ref, mu_rkv_ref, mu_wa_ref, mu_gd_ref, w0_ref, w2_ref, a0_ref, a2_ref,
                 g2_ref, kk_ref, ka_ref, rk_ref, gng_ref, gnb_ref, hsum_ref, tril_ref, o_ref,
                 ext_rkv, ext_wa, ext_gd, state_ref, obuf_ref):
    L = RWKV_CHUNK
    CH = RWKV_STEP_CHUNKS
    rows = L * CH
    N = RWKV_HEAD_DIM
    first = pl.program_id(1) == 0

    @pl.when(first)
    def _():
        state_ref[...] = jnp.zeros(state_ref.shape, F32)

    x = rkv_ref[...]
    xm = x + (_shifted(ext_rkv, x, first, rows) - x) * mu_rkv_ref[...]
    wa = wa_ref[...]
    wam = wa + (_shifted(ext_wa, wa, first, rows) - wa) * mu_wa_ref[...]
    gd = gd_ref[...]
    gdm = gd + (_shifted(ext_gd, gd, first, rows) - gd) * mu_gd_ref[...]

    r = xm[:, 0:RWKV_WIDTH]
    k = xm[:, RWKV_WIDTH:2 * RWKV_WIDTH]
    v = xm[:, 2 * RWKV_WIDTH:3 * RWKV_WIDTH]
    w = -_softplus(-(w0_ref[...] + _mm(jnp.tanh(wam), w2_ref[...], "x3"))) - 0.5
    logdec = -jnp.exp(w)
    a = _sigmoid(a0_ref[...] + _dot(wam, a2_ref[...]))
    g = _dot(_sigmoid(gdm), g2_ref[...])
    hsum = hsum_ref[...]
    kks = k * kk_ref[...]
    kk = kks / jnp.maximum(jnp.sqrt(_dot_onehot_rhs(kks * kks, hsum, 2)), 1e-12)
    k2 = k * (1.0 + (a - 1.0) * ka_ref[...])
    bonus = _dot_onehot_rhs(r * k2 * rk_ref[...], hsum, 2) * v
    beta = kk * a

    cum = _dot_onehot_lhs(tril_ref[...], logdec, 3)
    cum_l = jnp.concatenate(
        [jnp.broadcast_to(cum[c * L + L - 1:c * L + L, :], (L, RWKV_WIDTH)) for c in range(CH)], axis=0)
    e_to_end = jnp.exp(cum_l - cum)
    e_neg = jnp.exp(-cum)
    a_t = -kk * jnp.exp(cum - logdec)
    r_t = r * jnp.exp(cum)
    b_t = beta * e_neg
    k_t = k2 * e_neg
    b_p = beta * e_to_end
    k_p = k2 * e_to_end
    p_l = jnp.exp(cum_l)

    row = lax.broadcasted_iota(jnp.int32, (L, L), 0)
    col = lax.broadcasted_iota(jnp.int32, (L, L), 1)
    strict = col < row
    incl = col <= row
    mask2 = jnp.concatenate([strict, incl], axis=0)

    items = [(c, h) for c in range(CH) for h in range(RWKV_HEADS)]
    cut = lambda arr, c, h: arr[c * L:(c + 1) * L, h * N:(h + 1) * N]
    ah = {i: cut(a_t, *i) for i in items}
    rh = {i: cut(r_t, *i) for i in items}
    vh = {i: cut(v, *i) for i in items}
    bp = {i: cut(b_p, *i) for i in items}
    lhs = {i: jnp.concatenate([ah[i], rh[i]], axis=0) for i in items}
    q_b = {i: _mm_nt(lhs[i], cut(b_t, *i), RWKV_PREC_SCORE) for i in items}
    q_k = {i: jnp.where(mask2, _mm_nt(lhs[i], cut(k_t, *i), RWKV_PREC_SCORE), 0.0) for i in items}
    qkv = {i: _mm(q_k[i], vh[i], RWKV_PREC_SOLVE) for i in items}
    xs = {i: jnp.concatenate([ah[i], qkv[i][0:L]], axis=1) for i in items}
    m = {i: jnp.where(strict, q_b[i][0:L], 0.0) for i in items}
    step = 1
    while step < L:
        xs = {i: xs[i] + _mm(m[i], xs[i], RWKV_PREC_SOLVE) for i in items}
        step *= 2
        if step < L:
            m = {i: _mm(m[i], m[i], RWKV_PREC_SOLVE) for i in items}
    arx = {i: _mm(jnp.where(incl, q_b[i][L:2 * L], 0.0), xs[i], RWKV_PREC_SOLVE) for i in items}
    q_t = {i: rh[i] + arx[i][:, 0:N] for i in items}
    o_loc = {i: arx[i][:, N:2 * N] + qkv[i][L:2 * L] for i in items}
    gk = {i: _mm(xs[i][:, 0:N].T, bp[i], RWKV_PREC_STATE) for i in items}
    s_loc = {i: _mm(jnp.concatenate([xs[i][:, N:2 * N], vh[i]], axis=0).T,
                    jnp.concatenate([bp[i], cut(k_p, *i)], axis=0), RWKV_PREC_STATE) for i in items}
    state = [state_ref[h] for h in range(RWKV_HEADS)]
    for c in range(CH):
        for h in range(RWKV_HEADS):
            i = (c, h)
            o_h = _mm_nt(q_t[i], state[h], RWKV_PREC_STATE) + o_loc[i]
            state[h] = (state[h] * p_l[c * L:c * L + 1, h * N:(h + 1) * N]
                        + _mm(state[h], gk[i], RWKV_PREC_STATE) + s_loc[i])
            obuf_ref[c * L:(c + 1) * L, h * N:(h + 1) * N] = o_h
    for h in range(RWKV_HEADS):
        state_ref[h] = state[h]

    o = obuf_ref[...]
    oc = o - _dot_onehot_rhs(o, hsum, 2) * (1.0 / N)
    var = _dot_onehot_rhs(oc * oc, hsum, 2) * (1.0 / N)
    o_ref[...] = (oc * lax.rsqrt(var + RWKV_GN_EPS) * gng_ref[...] + gnb_ref[...] + bonus) * g


def _rwkv(p, prm, bsz, seq):
    L = RWKV_CHUNK * RWKV_STEP_CHUNKS
    tiles = seq // L
    n = bsz * seq
    row = lambda c: (lambda bi, i: (bi * tiles + i, c))
    fix = lambda bi, i: (0, 0)
    vec = lambda width: pl.BlockSpec((1, width), fix)
    in_specs = [pl.BlockSpec((L, 3 * RWKV_WIDTH), row(0)),
                pl.BlockSpec((L, LANES), row(20)), pl.BlockSpec((L, LANES), row(21)),
                vec(3 * RWKV_WIDTH), vec(LANES), vec(LANES),
                vec(RWKV_WIDTH), pl.BlockSpec((LANES, RWKV_WIDTH), fix),
                vec(RWKV_WIDTH), pl.BlockSpec((LANES, RWKV_WIDTH), fix),
                pl.BlockSpec((LANES, RWKV_WIDTH), fix),
                vec(RWKV_WIDTH), vec(RWKV_WIDTH), vec(RWKV_WIDTH), vec(RWKV_WIDTH), vec(RWKV_WIDTH),
                pl.BlockSpec((RWKV_WIDTH, RWKV_WIDTH), fix), pl.BlockSpec((L, L), fix)]
    return pl.pallas_call(
        _rwkv_kernel,
        grid=(bsz, tiles),
        in_specs=in_specs,
        out_specs=pl.BlockSpec((L, RWKV_WIDTH), lambda bi, i: (bi * tiles + i, 0)),
        out_shape=jax.ShapeDtypeStruct((n, RWKV_WIDTH), F32),
        scratch_shapes=[pltpu.VMEM((SUBLANES + L, 3 * RWKV_WIDTH), F32),
                        pltpu.VMEM((SUBLANES + L, LANES), F32),
                        pltpu.VMEM((SUBLANES + L, LANES), F32),
                        pltpu.VMEM((RWKV_HEADS, RWKV_HEAD_DIM, RWKV_HEAD_DIM), F32),
                        pltpu.VMEM((L, RWKV_WIDTH), F32)],
        compiler_params=_cparams(("arbitrary", "arbitrary"), 48 << 20),
    )(p, p, p, *prm)


def _causal_conv(ext_ref, x, w_ref, b_ref, first, rows, taps, halo, win_ref=None):
    @pl.when(first)
    def _():
        ext_ref[0:halo, :] = jnp.zeros((halo, x.shape[1]), F32)

    ext_ref[halo:halo + rows, :] = x
    acc = x * w_ref[taps - 1:taps, :] + b_ref[...]
    for b in range(min(SUBLANES, taps)):
        n_a = (taps - 1 - b) // SUBLANES + 1
        span = SUBLANES * (n_a - 1)
        if win_ref is not None and b > 0:
            win_ref[0:span + rows, :] = ext_ref[halo - span - b:halo - b + rows, :]
        for a in range(n_a):
            lag = SUBLANES * a + b
            if lag == 0:
                continue
            if win_ref is None or b == 0:
                tap = ext_ref[halo - lag:halo - lag + rows, :]
            else:
                tap = win_ref[span - SUBLANES * a:span - SUBLANES * a + rows, :]
            acc = acc + tap * w_ref[taps - 1 - lag:taps - lag, :]
    ext_ref[0:halo, :] = x[rows - halo:rows, :]
    return acc


def _lru_kernel(xb_ref, gb_ref, cw_ref, cb_ref, wa_ref, ba_ref, wx_ref, bx_ref, lam_ref, o_ref,
                ext_ref, carry_ref):
    rows = LRU_TILE
    first = pl.program_id(1) == 0

    @pl.when(first)
    def _():
        carry_ref[...] = jnp.zeros(carry_ref.shape, F32)

    xc = _causal_conv(ext_ref, xb_ref[...], cw_ref, cb_ref, first, rows, LRU_CONV, SUBLANES)
    gate_r = _sigmoid(_dot(xc, wa_ref[...]) + ba_ref[...])
    gate_i = _sigmoid(_dot(xc, wx_ref[...]) + bx_ref[...])
    log_a = -LRU_C * _softplus(-lam_ref[...]) * gate_r
    a = jnp.exp(log_a)
    u = jnp.sqrt(1.0 - jnp.exp(2.0 * log_a)) * (gate_i * xc)
    ridx = lax.broadcasted_iota(jnp.int32, (rows, LRU_WIDTH), 0)
    s = 1
    while s < rows:
        keep = ridx >= s
        a_sh = jnp.where(keep, pltpu.roll(a, s, axis=0), 1.0)
        u_sh = jnp.where(keep, pltpu.roll(u, s, axis=0), 0.0)
        u = a * u_sh + u
        a = a * a_sh
        s *= 2
    h = a * carry_ref[0:1, :] + u
    carry_ref[...] = jnp.broadcast_to(h[rows - 1:rows, :], carry_ref.shape)
    o_ref[...] = h * _gelu_tanh(gb_ref[...])


def _lru(p, prm, bsz, seq):
    rows = LRU_TILE
    tiles = seq // rows
    n = bsz * seq
    row = lambda c: (lambda bi, i: (bi * tiles + i, c))
    fix = lambda bi, i: (0, 0)
    vec = pl.BlockSpec((1, LRU_WIDTH), fix)
    mat = pl.BlockSpec((LRU_WIDTH, LRU_WIDTH), fix)
    return pl.pallas_call(
        _lru_kernel,
        grid=(bsz, tiles),
        in_specs=[pl.BlockSpec((rows, LRU_WIDTH), row(3)), pl.BlockSpec((rows, LRU_WIDTH), row(4)),
                  pl.BlockSpec((LRU_CONV, LRU_WIDTH), fix), vec, mat, vec, mat, vec, vec],
        out_specs=pl.BlockSpec((rows, LRU_WIDTH), lambda bi, i: (bi * tiles + i, 0)),
        out_shape=jax.ShapeDtypeStruct((n, LRU_WIDTH), F32),
        scratch_shapes=[pltpu.VMEM((SUBLANES + rows, LRU_WIDTH), F32),
                        pltpu.VMEM((SUBLANES, LRU_WIDTH), F32)],
        compiler_params=_cparams(("arbitrary", "arbitrary"), 32 << 20),
    )(p, p, *prm)


def _conf_kernel(c_ref, cw_ref, cb_ref, g_ref, b_ref, o_ref, ext_ref, win_ref):
    rows = CONF_TILE
    first = pl.program_id(1) == 0
    c = c_ref[...]
    glu = c[:, 0:CONF_WIDTH] * _sigmoid(c[:, CONF_WIDTH:2 * CONF_WIDTH])
    u = _causal_conv(ext_ref, glu, cw_ref, cb_ref, first, rows, CONF_CONV, CONF_HALO, win_ref)
    mu = jnp.mean(u, axis=-1, keepdims=True)
    uc = u - mu
    var = jnp.mean(uc * uc, axis=-1, keepdims=True)
    o_ref[...] = _silu(uc * lax.rsqrt(var + LN_EPS) * g_ref[...] + b_ref[...])


def _conf(p, prm, bsz, seq):
    rows = CONF_TILE
    tiles = seq // rows
    n = bsz * seq
    fix = lambda bi, i: (0, 0)
    vec = pl.BlockSpec((1, CONF_WIDTH), fix)
    return pl.pallas_call(
        _conf_kernel,
        grid=(bsz, tiles),
        in_specs=[pl.BlockSpec((rows, 2 * CONF_WIDTH), lambda bi, i: (bi * tiles + i, 0)),
                  pl.BlockSpec((CONF_CONV, CONF_WIDTH), fix), vec, vec, vec],
        out_specs=pl.BlockSpec((rows, CONF_WIDTH), lambda bi, i: (bi * tiles + i, 0)),
        out_shape=jax.ShapeDtypeStruct((n, CONF_WIDTH), F32),
        scratch_shapes=[pltpu.VMEM((CONF_HALO + rows, CONF_WIDTH), F32),
                        pltpu.VMEM((CONF_HALO + rows, CONF_WIDTH), F32)],
        compiler_params=_cparams(("arbitrary", "arbitrary"), 32 << 20),
    )(p, *prm)


def _ssd_kernel(z_ref, xs_ref, bc_ref, dt_ref, cwx_ref, cbx_ref, cwb_ref, cbb_ref, dtb_ref, alog_ref,
                dskip_ref, normg_ref, expand_ref, tril_ref, o_ref, ext_xs, ext_bc, state_ref, ybuf_ref):
    L = SSM_CHUNK
    P = SSM_HEAD_DIM
    NS = SSM_STATE
    HG = SSM_HEADS // SSM_GROUPS
    first = pl.program_id(1) == 0

    @pl.when(first)
    def _():
        state_ref[...] = jnp.zeros(state_ref.shape, F32)

    xs = _silu(_causal_conv(ext_xs, xs_ref[...], cwx_ref, cbx_ref, first, L, SSM_CONV, SUBLANES))
    bc = _silu(_causal_conv(ext_bc, bc_ref[...], cwb_ref, cbb_ref, first, L, SSM_CONV, SUBLANES))
    dt = _softplus(dt_ref[...] + dtb_ref[...])
    a_head = -jnp.exp(alog_ref[...])
    acum = _dot_onehot_lhs(tril_ref[...], dt * a_head, 3)
    acum_t = acum.T
    acum_x = _dot_onehot_rhs(acum, expand_ref[...], 3)
    acum_xl = acum_x[L - 1:L, :]
    xdt = xs * _dot_onehot_rhs(dt, expand_ref[...], 3)
    causal = lax.broadcasted_iota(jnp.int32, (L, L), 1) <= lax.broadcasted_iota(jnp.int32, (L, L), 0)

    half = SSM_INNER // SSM_GROUPS
    x_end = (xdt * jnp.exp(acum_xl - acum_x)).astype(BF16)
    state = state_ref[...]
    cb = []
    for g in range(SSM_GROUPS):
        gs = slice(g * half, (g + 1) * half)
        bm = bc[:, g * NS:(g + 1) * NS]
        cm = bc[:, (SSM_GROUPS + g) * NS:(SSM_GROUPS + g + 1) * NS].astype(BF16)
        cb.append(_dot_nt(cm, bm))
        ybuf_ref[:, gs] = _dot(cm, state[:, gs])
        state_ref[:, gs] = state[:, gs] * jnp.exp(acum_xl[:, gs]) + _dot(bm.T, x_end[:, gs])
    y_off = ybuf_ref[...] * jnp.exp(acum_x)
    for h0 in range(0, SSM_HEADS, SSM_HEAD_BATCH):
        hs = range(h0, h0 + SSM_HEAD_BATCH)
        decay = {h: jnp.exp(jnp.where(causal, acum[:, h:h + 1] - acum_t[h:h + 1, :], -jnp.inf)) for h in hs}
        y_diag = {h: _dot(cb[h // HG] * decay[h], xdt[:, h * P:(h + 1) * P]) for h in hs}
        for h in hs:
            ybuf_ref[:, h * P:(h + 1) * P] = y_diag[h]

    y = (ybuf_ref[...] + y_off + dskip_ref[...] * xs) * _silu(z_ref[...])
    for g in range(SSM_GROUPS):
        yg = y[:, g * half:(g + 1) * half]
        ms = jnp.mean(yg * yg, axis=-1, keepdims=True)
        o_ref[:, g * half:(g + 1) * half] = yg * lax.rsqrt(ms + LN_EPS) * normg_ref[:, g * half:(g + 1) * half]


def _ssd(p, prm, bsz, seq):
    L = SSM_CHUNK
    tiles = seq // L
    n = bsz * seq
    row = lambda c: (lambda bi, i: (bi * tiles + i, c))
    fix = lambda bi, i: (0, 0)
    bcw = 2 * SSM_GROUPS * SSM_STATE
    vec = lambda width: pl.BlockSpec((1, width), fix)
    in_specs = [pl.BlockSpec((L, SSM_INNER), row(1)), pl.BlockSpec((L, SSM_INNER), row(2)),
                pl.BlockSpec((L, bcw), row(6)), pl.BlockSpec((L, LANES), row(28)),
                pl.BlockSpec((SSM_CONV, SSM_INNER), fix), vec(SSM_INNER),
                pl.BlockSpec((SSM_CONV, bcw), fix), vec(bcw),
                vec(LANES), vec(LANES), vec(SSM_INNER), vec(SSM_INNER),
                pl.BlockSpec((LANES, SSM_INNER), fix), pl.BlockSpec((L, L), fix)]
    return pl.pallas_call(
        _ssd_kernel,
        grid=(bsz, tiles),
        in_specs=in_specs,
        out_specs=pl.BlockSpec((L, SSM_INNER), lambda bi, i: (bi * tiles + i, 0)),
        out_shape=jax.ShapeDtypeStruct((n, SSM_INNER), F32),
        scratch_shapes=[pltpu.VMEM((SUBLANES + L, SSM_INNER), F32),
                        pltpu.VMEM((SUBLANES + L, bcw), F32),
                        pltpu.VMEM((SSM_STATE, SSM_INNER), F32),
                        pltpu.VMEM((L, SSM_INNER), F32)],
        compiler_params=_cparams(("arbitrary", "arbitrary"), 32 << 20),
    )(p, p, p, p, *prm)


def _row(v):
    return v.reshape(1, -1).astype(F32)


def _pad_rows(m, rows, offset):
    out = jnp.zeros((rows, m.shape[1]), F32)
    return out.at[offset:offset + m.shape[0]].set(m)


def _block_diag(w):
    nb, d, e = w.shape
    eye = jnp.eye(nb, dtype=w.dtype)
    return (eye[:, None, :, None] * w[:, :, None, :]).reshape(nb * d, nb * e)


def _tril_ones(n, blocks=1):
    return jnp.asarray(np.kron(np.eye(blocks, dtype=np.float32), np.tril(np.ones((n, n), np.float32))), dtype=BF16)


def _head_sum_matrix():
    idx = np.arange(RWKV_WIDTH) // RWKV_HEAD_DIM
    return jnp.asarray((idx[:, None] == idx[None, :]).astype(np.float32), dtype=BF16)


def _head_expand_matrix():
    m = np.zeros((LANES, SSM_INNER), np.float32)
    for h in range(SSM_HEADS):
        m[h, h * SSM_HEAD_DIM:(h + 1) * SSM_HEAD_DIM] = 1.0
    return jnp.asarray(m, dtype=BF16)


def _sublayers(h, mem2, i, bsz, xa_wq, xa_wk, xa_wv, xa_wo, ffn_w1, ffn_w2, ln_mem_g, ln_mem_b, ln_ffn_g, ln_ffn_b):
    kmem = _proj(mem2, xa_wk[i].astype(BF16), mem2.shape[0] // bsz, BF16)
    vmem_ = _proj(mem2, xa_wv[i].astype(BF16), mem2.shape[0] // bsz, BF16)
    h = _attn(h, kmem, vmem_, xa_wq[i].astype(BF16), xa_wo[i].astype(BF16), _row(ln_mem_g[i]), _row(ln_mem_b[i]),
              bsz, 512)
    return _ffn(h, ffn_w1[i].astype(BF16), ffn_w2[i].astype(BF16), _row(ln_ffn_g[i]), _row(ln_ffn_b[i]), 512)


def kernel(x, mem, ev_w_in, ev_mu, ev_w0, ev_w2, ev_a0, ev_a2, ev_g2, ev_k_k, ev_k_a, ev_r_k, ev_gn_g, ev_gn_b, ev_lru_conv_w, ev_lru_conv_b, ev_lru_wa, ev_lru_ba, ev_lru_wx, ev_lru_bx, ev_lru_lam, ev_w_out, od_w_in, od_cf_conv_w, od_cf_conv_b, od_cf_ln_g, od_cf_ln_b, od_ssm_conv_w, od_ssm_conv_b, od_dt_bias, od_a_log, od_d_skip, od_ssm_norm_g, od_w_out, xa_wq, xa_wk, xa_wv, xa_wo, ffn_w1, ffn_w2, ln_mix_g, ln_mix_b, ln_mem_g, ln_mem_b, ln_ffn_g, ln_ffn_b):
    bsz, seq, _ = x.shape
    n = bsz * seq
    h = x.reshape(n, D_MODEL)
    mem2 = mem.reshape(bsz * mem.shape[1], D_MODEL)
    xa = (xa_wq, xa_wk, xa_wv, xa_wo, ffn_w1, ffn_w2, ln_mem_g, ln_mem_b, ln_ffn_g, ln_ffn_b)

    rw = 3 * RWKV_WIDTH
    lo = 2 * LANES
    w_in = jnp.concatenate([ev_w_in[0][:, 0:rw], ev_w_in[0][:, rw + lo:], ev_w_in[0][:, rw:rw + lo]], axis=1)
    p = _proj(h, w_in.astype(BF16), 512)
    mu = ev_mu[0]
    rwkv_prm = (_row(mu[0:rw]), _row(mu[rw:rw + LANES]), _row(mu[rw + LANES:rw + lo]),
                _row(ev_w0[0]), _pad_rows(ev_w2[0], LANES, 0), _row(ev_a0[0]), _pad_rows(ev_a2[0], LANES, 64),
                ev_g2[0].astype(F32), _row(ev_k_k[0]), _row(ev_k_a[0]), _row(ev_r_k[0]), _row(ev_gn_g[0]),
                _row(ev_gn_b[0]), _head_sum_matrix(), _tril_ones(RWKV_CHUNK, RWKV_STEP_CHUNKS))
    y_a = _rwkv(p, rwkv_prm, bsz, seq)
    lru_prm = (ev_lru_conv_w[0], _row(ev_lru_conv_b[0]), _block_diag(ev_lru_wa[0]).astype(BF16), _row(ev_lru_ba[0]),
               _block_diag(ev_lru_wx[0]).astype(BF16), _row(ev_lru_bx[0]), _row(ev_lru_lam[0]))
    y_b = _lru(p, lru_prm, bsz, seq)
    w_out = ev_w_out[0].astype(BF16)
    h = _mix_out(y_a, y_b, w_out[0:RWKV_WIDTH], w_out[RWKV_WIDTH:], h, _row(ln_mix_g[0]), _row(ln_mix_b[0]), 512)
    h = _sublayers(h, mem2, 0, bsz, *xa)

    w_in = jnp.pad(od_w_in[0], ((0, 0), (0, ODD_IN_PAD - ODD_IN))).astype(BF16)
    p = _proj(h, w_in, 256)
    conf_prm = (od_cf_conv_w[0], _row(od_cf_conv_b[0]), _row(od_cf_ln_g[0]), _row(od_cf_ln_b[0]))
    y_c = _conf(p, conf_prm, bsz, seq)
    cw, cb = od_ssm_conv_w[0], od_ssm_conv_b[0]
    pad16 = lambda v: jnp.pad(v, (0, LANES - SSM_HEADS)).reshape(1, LANES).astype(F32)
    ssd_prm = (cw[:, 0:SSM_INNER], _row(cb[0:SSM_INNER]), cw[:, SSM_INNER:], _row(cb[SSM_INNER:]),
               pad16(od_dt_bias[0]), pad16(od_a_log[0]), _row(jnp.repeat(od_d_skip[0], SSM_HEAD_DIM)),
               _row(od_ssm_norm_g[0]), _head_expand_matrix(), _tril_ones(SSM_CHUNK))
    y_d = _ssd(p, ssd_prm, bsz, seq)
    w_out = od_w_out[0].astype(BF16)
    h = _mix_out(y_c, y_d, w_out[0:CONF_WIDTH], w_out[CONF_WIDTH:], h, _row(ln_mix_g[1]), _row(ln_mix_b[1]), 512)
    h = _sublayers(h, mem2, 1, bsz, *xa)
    return h.reshape(bsz, seq, D_MODEL)
```

```python
import functools

import jax
import jax.numpy as jnp
import numpy as np
from jax import lax
from jax.experimental import pallas as pl
from jax.experimental.pallas import tpu as pltpu

F32 = jnp.float32
BF16 = jnp.bfloat16
HI = lax.Precision.HIGHEST

D_MODEL = 1024
DEPTH = 2
DN_ALPHA = (2 * DEPTH) ** 0.25
LN_EPS = 1e-5

RWKV_WIDTH = 512
RWKV_HEAD_DIM = 64
RWKV_HEADS = 8
RWKV_GN_EPS = 64e-5
RWKV_CHUNK = 64
RWKV_STEP_CHUNKS = 4
RWKV_GROUP = 4

LRU_WIDTH = 512
LRU_C = 8.0
LRU_CONV = 4
LRU_TILE = 256

CONF_WIDTH = 512
CONF_CONV = 31
CONF_HALO = 32
CONF_TILE = 256

SSM_INNER = 1024
SSM_HEAD_DIM = 64
SSM_HEADS = 16
SSM_GROUPS = 2
SSM_STATE = 128
SSM_CONV = 4
SSM_CHUNK = 128
SSM_HEAD_BATCH = 4
ODD_IN = 3600
ODD_IN_PAD = 3712

EPILOGUE_SPLIT = 2
XA_HEADS = 4
XA_HEAD_DIM = 256
D_FF = 4096

SUBLANES = 8
LANES = 128
VMEM_CAP = 56 * 1024 * 1024


def _cparams(semantics, vmem_bytes):
    return pltpu.CompilerParams(dimension_semantics=semantics,
                                vmem_limit_bytes=int(min(max(vmem_bytes, 16 * 1024 * 1024), VMEM_CAP)))


def _dot(a, b):
    return jnp.dot(a.astype(BF16), b.astype(BF16), preferred_element_type=F32)


def _split_parts(x, parts):
    out = []
    for _ in range(parts - 1):
        hi = x.astype(BF16)
        out.append(hi)
        x = x - hi.astype(F32)
    out.append(x.astype(BF16))
    return out


def _dot_onehot_rhs(x, m, parts):
    acc = None
    for xp in _split_parts(x, parts):
        t = jnp.dot(xp, m, preferred_element_type=F32)
        acc = t if acc is None else acc + t
    return acc


def _dot_onehot_lhs(m, x, parts):
    acc = None
    for xp in _split_parts(x, parts):
        t = jnp.dot(m, xp, preferred_element_type=F32)
        acc = t if acc is None else acc + t
    return acc


def _head_sums(terms, bdm):
    pieces, owner = [], []
    for n, (x, parts) in enumerate(terms):
        for p in _split_parts(x, parts):
            pieces.append(p)
            owner.append(n)
    rows = terms[0][0].shape[0]
    w = bdm.shape[0]
    stacked = jnp.concatenate(pieces, axis=0)
    sums = jnp.concatenate([jnp.dot(stacked[:, q * w:(q + 1) * w], bdm, preferred_element_type=F32)
                            for q in range(stacked.shape[1] // w)], axis=1)
    out = [None] * len(terms)
    for j, n in enumerate(owner):
        piece = sums[j * rows:(j + 1) * rows]
        out[n] = piece if out[n] is None else out[n] + piece
    return out


def _dot_nt(a, b):
    return lax.dot_general(a.astype(BF16), b.astype(BF16), (((1,), (1,)), ((), ())),
                           preferred_element_type=F32)


def _split_bf16(x):
    hi = x.astype(BF16)
    return hi, (x - hi.astype(F32)).astype(BF16)


def _mm_dims(a, b, dims, mode):
    if mode == "hi":
        return lax.dot_general(a, b, dims, preferred_element_type=F32, precision=HI)
    if mode == "bf":
        return lax.dot_general(a.astype(BF16), b.astype(BF16), dims, preferred_element_type=F32)
    a_hi, a_lo = _split_bf16(a)
    b_hi, b_lo = _split_bf16(b)
    d = lambda u, w: lax.dot_general(u, w, dims, preferred_element_type=F32)
    return d(a_hi, b_hi) + (d(a_hi, b_lo) + d(a_lo, b_hi))


def _mm(a, b, mode):
    return _mm_dims(a, b, (((1,), (0,)), ((), ())), mode)


def _mm_nt(a, b, mode):
    return _mm_dims(a, b, (((1,), (1,)), ((), ())), mode)


def _sigmoid(x):
    return 1.0 / (1.0 + jnp.exp(-x))


def _softplus(x):
    return jnp.maximum(x, 0.0) + jnp.log(1.0 + jnp.exp(-jnp.abs(x)))


def _silu(x):
    return x * _sigmoid(x)


def _gelu_tanh(x):
    return 0.5 * x * (1.0 + jnp.tanh(0.7978845608028654 * (x + 0.044715 * (x * x * x))))


def _res_ln(h, y, g, b):
    z = DN_ALPHA * h + y
    mu = jnp.mean(z, axis=-1, keepdims=True)
    zc = z - mu
    var = jnp.mean(zc * zc, axis=-1, keepdims=True)
    return zc * lax.rsqrt(var + LN_EPS) * g + b


def _proj_kernel(x_ref, w_ref, o_ref):
    o_ref[...] = _dot(x_ref[...], w_ref[...]).astype(o_ref.dtype)


def _proj(x, w, tm, out_dtype=F32):
    n, k = x.shape
    m = w.shape[1]
    vmem = 2 * (tm * k * x.dtype.itemsize + k * m * 2 + tm * m * jnp.dtype(out_dtype).itemsize) + (4 << 20)
    return pl.pallas_call(
        _proj_kernel,
        grid=(n // tm,),
        in_specs=[pl.BlockSpec((tm, k), lambda i: (i, 0)), pl.BlockSpec((k, m), lambda i: (0, 0))],
        out_specs=pl.BlockSpec((tm, m), lambda i: (i, 0)),
        out_shape=jax.ShapeDtypeStruct((n, m), out_dtype),
        compiler_params=_cparams(("parallel",), vmem),
    )(x, w)


def _mix_out_kernel(a_ref, b_ref, wa_ref, wb_ref, h_ref, g_ref, bias_ref, o_ref):
    rows = o_ref.shape[0] // EPILOGUE_SPLIT
    ys = []
    for s in range(EPILOGUE_SPLIT):
        sl = slice(s * rows, (s + 1) * rows)
        ys.append(_dot(a_ref[sl, :], wa_ref[...]) + _dot(b_ref[sl, :], wb_ref[...]))
    for s in range(EPILOGUE_SPLIT):
        sl = slice(s * rows, (s + 1) * rows)
        o_ref[sl, :] = _res_ln(h_ref[sl, :], ys[s], g_ref[...], bias_ref[...])


def _mix_out(a, b, wa, wb, h, g, bias, tm):
    n = h.shape[0]
    ka, kb = a.shape[1], b.shape[1]
    vmem = 2 * (tm * (ka + kb + 2 * D_MODEL) * 4 + (ka + kb) * D_MODEL * 2) + (8 << 20)
    row = lambda i: (i, 0)
    fix = lambda i: (0, 0)
    return pl.pallas_call(
        _mix_out_kernel,
        grid=(n // tm,),
        in_specs=[pl.BlockSpec((tm, ka), row), pl.BlockSpec((tm, kb), row),
                  pl.BlockSpec((ka, D_MODEL), fix), pl.BlockSpec((kb, D_MODEL), fix),
                  pl.BlockSpec((tm, D_MODEL), row), pl.BlockSpec((1, D_MODEL), fix),
                  pl.BlockSpec((1, D_MODEL), fix)],
        out_specs=pl.BlockSpec((tm, D_MODEL), row),
        out_shape=jax.ShapeDtypeStruct((n, D_MODEL), F32),
        compiler_params=_cparams(("parallel",), vmem),
    )(a, b, wa, wb, h, g, bias)


def _attn_kernel(h_ref, k_ref, v_ref, wq_ref, wo_ref, g_ref, b_ref, o_ref, cat_ref):
    rows = o_ref.shape[0] // EPILOGUE_SPLIT
    subs = [slice(s * rows, (s + 1) * rows) for s in range(EPILOGUE_SPLIT)]
    q = [_dot(h_ref[rs, :], wq_ref[...]) for rs in subs]
    for hd in range(XA_HEADS):
        sl = slice(hd * XA_HEAD_DIM, (hd + 1) * XA_HEAD_DIM)
        for n, rs in enumerate(subs):
            s = _dot_nt(q[n][:, sl], k_ref[:, sl]) * (XA_HEAD_DIM ** -0.5)
            s = s - jnp.max(s, axis=-1, keepdims=True)
            e = jnp.exp(s)
            p = e / jnp.sum(e, axis=-1, keepdims=True)
            cat_ref[rs, sl] = _dot(p, v_ref[:, sl])
    y = [_dot(cat_ref[rs, :], wo_ref[...]) for rs in subs]
    for n, rs in enumerate(subs):
        o_ref[rs, :] = _res_ln(h_ref[rs, :], y[n], g_ref[...], b_ref[...])


def _attn(h, kmem, vmem_, wq, wo, g, b, bsz, tm):
    n = h.shape[0]
    tiles = n // bsz // tm
    n_mem = kmem.shape[0] // bsz
    vmem = 2 * (2 * tm * D_MODEL * 4 + 2 * n_mem * D_MODEL * 2 + 2 * D_MODEL * D_MODEL * 2) + tm * D_MODEL * 16 + (8 << 20)
    row = lambda bi, i: (bi * tiles + i, 0)
    mem = lambda bi, i: (bi, 0)
    fix = lambda bi, i: (0, 0)
    return pl.pallas_call(
        _attn_kernel,
        grid=(bsz, tiles),
        in_specs=[pl.BlockSpec((tm, D_MODEL), row), pl.BlockSpec((n_mem, D_MODEL), mem),
                  pl.BlockSpec((n_mem, D_MODEL), mem), pl.BlockSpec((D_MODEL, D_MODEL), fix),
                  pl.BlockSpec((D_MODEL, D_MODEL), fix), pl.BlockSpec((1, D_MODEL), fix),
                  pl.BlockSpec((1, D_MODEL), fix)],
        out_specs=pl.BlockSpec((tm, D_MODEL), row),
        out_shape=jax.ShapeDtypeStruct((n, D_MODEL), F32),
        scratch_shapes=[pltpu.VMEM((tm, D_MODEL), F32)],
        compiler_params=_cparams(("parallel", "parallel"), vmem),
    )(h, kmem, vmem_, wq, wo, g, b)


def _ffn_kernel(h_ref, w1_ref, w2_ref, g_ref, b_ref, o_ref):
    rows = o_ref.shape[0] // EPILOGUE_SPLIT
    subs = [slice(s * rows, (s + 1) * rows) for s in range(EPILOGUE_SPLIT)]
    u = [jnp.square(jnp.maximum(_dot(h_ref[rs, :], w1_ref[...]), 0.0)).astype(BF16) for rs in subs]
    y = [_dot(u[n], w2_ref[...]) for n in range(EPILOGUE_SPLIT)]
    for n, rs in enumerate(subs):
        o_ref[rs, :] = _res_ln(h_ref[rs, :], y[n], g_ref[...], b_ref[...])


def _ffn(h, w1, w2, g, b, tm):
    n = h.shape[0]
    vmem = 2 * D_MODEL * D_FF * 2 + 4 * tm * D_MODEL * 4 + tm * D_FF * 6 + tm * D_MODEL * 6 + (6 << 20)
    fix = lambda i: (0, 0)
    return pl.pallas_call(
        _ffn_kernel,
        grid=(n // tm,),
        in_specs=[pl.BlockSpec((tm, D_MODEL), lambda i: (i, 0)),
                  pl.BlockSpec((D_MODEL, D_FF), fix, pipeline_mode=pl.Buffered(1)),
                  pl.BlockSpec((D_FF, D_MODEL), fix, pipeline_mode=pl.Buffered(1)),
                  pl.BlockSpec((1, D_MODEL), fix),
                  pl.BlockSpec((1, D_MODEL), fix)],
        out_specs=pl.BlockSpec((tm, D_MODEL), lambda i: (i, 0)),
        out_shape=jax.ShapeDtypeStruct((n, D_MODEL), F32),
        compiler_params=_cparams(("parallel",), vmem),
    )(h, w1, w2, g, b)


def _shifted(ext_ref, x, first, rows):
    @pl.when(first)
    def _():
        ext_ref[0:SUBLANES, :] = jnp.zeros((SUBLANES, x.shape[1]), F32)

    ext_ref[SUBLANES:SUBLANES + rows, :] = x
    prev = ext_ref[SUBLANES - 1:SUBLANES - 1 + rows, :]
    ext_ref[0:SUBLANES, :] = x[rows - SUBLANES:rows, :]
    return prev


def _rwkv_kernel(rkv_ref, wa_ref, gd_ref, mu_rkv_ref, mu_wa_ref, mu_gd_ref, w0_ref, w2_ref, a0_ref, a2_ref,
                 g2_ref, kk_ref, ka_ref, rk_ref, gng_ref, gnb_ref, bdm_ref, tril_ref, o_ref,
                 ext_rkv, ext_wa, ext_gd, state_ref, obuf_ref):
    L = RWKV_CHUNK
    CH = RWKV_STEP_CHUNKS
    rows = L * CH
    N = RWKV_HEAD_DIM
    first = pl.program_id(1) == 0

    @pl.when(first)
    def _():
        state_ref[...] = jnp.zeros(state_ref.shape, F32)

    x = rkv_ref[...]
    xm = x + (_shifted(ext_rkv, x, first, rows) - x) * mu_rkv_ref[...]
    wa = wa_ref[...]
    wam = wa + (_shifted(ext_wa, wa, first, rows) - wa) * mu_wa_ref[...]
    gd = gd_ref[...]
    gdm = gd + (_shifted(ext_gd, gd, first, rows) - gd) * mu_gd_ref[...]

    r = xm[:, 0:RWKV_WIDTH]
    k = xm[:, RWKV_WIDTH:2 * RWKV_WIDTH]
    v = xm[:, 2 * RWKV_WIDTH:3 * RWKV_WIDTH]
    w = -_softplus(-(w0_ref[...] + _dot(jnp.tanh(wam), w2_ref[...]))) - 0.5
    logdec = -jnp.exp(w)
    a = _sigmoid(a0_ref[...] + _dot(wam, a2_ref[...]))
    g = _dot(_sigmoid(gdm), g2_ref[...])
    bdm = bdm_ref[...]
    kks = k * kk_ref[...]
    k2 = k * (1.0 + (a - 1.0) * ka_ref[...])
    kk_sq, bonus_s = _head_sums([(kks * kks, 2), (r * k2 * rk_ref[...], 1)], bdm)
    kk = kks / jnp.maximum(jnp.sqrt(kk_sq), 1e-12)
    bonus = bonus_s * v
    beta = kk * a

    cum = _dot_onehot_lhs(tril_ref[...], logdec, 2)
    cum_l = jnp.concatenate(
        [jnp.broadcast_to(cum[c * L + L - 1:c * L + L, :], (L, RWKV_WIDTH)) for c in range(CH)], axis=0)
    e_to_end = jnp.exp(cum_l - cum)
    e_neg = jnp.exp(-cum)
    a_t = -kk * jnp.exp(cum - logdec)
    r_t = r * jnp.exp(cum)
    b_t = beta * e_neg
    k_t = k2 * e_neg
    b_p = beta * e_to_end
    k_p = k2 * e_to_end
    p_l = jnp.exp(cum_l)

    W = RWKV_GROUP * N
    bdm32 = bdm.astype(F32)
    lane_s = jnp.bitwise_and(lax.broadcasted_iota(jnp.int32, (L, W), 1), N - 1)
    t_idx = lax.broadcasted_iota(jnp.int32, (L, W), 0)
    strict = lane_s < t_idx
    incl = lane_s <= t_idx
    eye_cat = jnp.where(lane_s == t_idx, 1.0, 0.0)
    mask2 = jnp.concatenate([strict, incl], axis=0)

    def bd(y):
        return jnp.concatenate([y.astype(BF16)] * RWKV_GROUP, axis=0) * bdm

    mmb = lambda x, y: jnp.dot(x.astype(BF16), y, preferred_element_type=F32)
    mmb_nt = lambda x, y: lax.dot_general(x.astype(BF16), y, (((1,), (1,)), ((), ())), preferred_element_type=F32)

    items = [(c, q) for c in range(CH) for q in range(RWKV_WIDTH // W)]
    cut = lambda arr, c, q: arr[c * L:(c + 1) * L, q * W:(q + 1) * W]
    a_c = {i: cut(a_t, *i) for i in items}
    r_c = {i: cut(r_t, *i) for i in items}
    v_c = {i: cut(v, *i) for i in items}
    bp_c = {i: cut(b_p, *i) for i in items}
    lhs = {i: jnp.concatenate([a_c[i], r_c[i]], axis=0).astype(BF16) for i in items}
    q_b = {i: mmb_nt(lhs[i], bd(cut(b_t, *i))) for i in items}
    q_k = {i: jnp.where(mask2, mmb_nt(lhs[i], bd(cut(k_t, *i))), 0.0) for i in items}
    qkv = {i: mmb(q_k[i], bd(v_c[i])) for i in items}
    pw = {i: jnp.where(strict, q_b[i][0:L], 0.0) for i in items}
    tinv = {i: eye_cat + pw[i] for i in items}
    pw = {i: mmb(pw[i], bd(pw[i])) for i in items}
    for _ in range(L.bit_length() - 3):
        both = {i: mmb(jnp.concatenate([pw[i], tinv[i]], axis=0), bd(pw[i])) for i in items}
        pw = {i: both[i][0:L] for i in items}
        tinv = {i: tinv[i] + both[i][L:2 * L] for i in items}
    tinv = {i: tinv[i] + mmb(tinv[i], bd(pw[i])) for i in items}
    ta = {i: mmb(tinv[i], bd(a_c[i])) for i in items}
    w2 = {i: mmb(tinv[i], bd(qkv[i][0:L])) for i in items}
    a_rb = {i: jnp.where(incl, q_b[i][L:2 * L], 0.0).astype(BF16) for i in items}
    q_t = {i: r_c[i] + mmb(a_rb[i], bd(ta[i])) for i in items}
    o_loc = {i: mmb(a_rb[i], bd(w2[i])) + qkv[i][L:2 * L] for i in items}
    gk = {i: _dot(ta[i].T, bp_c[i]).astype(BF16) * bdm for i in items}
    s_loc = {i: _dot(jnp.concatenate([w2[i], v_c[i]], axis=0).T,
                     jnp.concatenate([bp_c[i], cut(k_p, *i)], axis=0)) * bdm32 for i in items}
    state = [state_ref[q] for q in range(RWKV_WIDTH // W)]
    for c in range(CH):
        for q in range(RWKV_WIDTH // W):
            i = (c, q)
            obuf_ref[c * L:(c + 1) * L, q * W:(q + 1) * W] = mmb_nt(q_t[i], state[q].astype(BF16)) + o_loc[i]
            state[q] = state[q] * p_l[c * L:c * L + 1, q * W:(q + 1) * W] + mmb(state[q], gk[i]) + s_loc[i]
    for q in range(RWKV_WIDTH // W):
        state_ref[q] = state[q]

    o = obuf_ref[...]
    oc = o - _head_sums([(o, 1)], bdm)[0] * (1.0 / N)
    var = _head_sums([(oc * oc, 1)], bdm)[0] * (1.0 / N)
    o_ref[...] = (oc * lax.rsqrt(var + RWKV_GN_EPS) * gng_ref[...] + gnb_ref[...] + bonus) * g


def _rwkv(p, prm, bsz, seq):
    L = RWKV_CHUNK * RWKV_STEP_CHUNKS
    tiles = seq // L
    n = bsz * seq
    row = lambda c: (lambda bi, i: (bi * tiles + i, c))
    fix = lambda bi, i: (0, 0)
    vec = lambda width: pl.BlockSpec((1, width), fix)
    in_specs = [pl.BlockSpec((L, 3 * RWKV_WIDTH), row(0)),
                pl.BlockSpec((L, LANES), row(20)), pl.BlockSpec((L, LANES), row(21)),
                vec(3 * RWKV_WIDTH), vec(LANES), vec(LANES),
                vec(RWKV_WIDTH), pl.BlockSpec((LANES, RWKV_WIDTH), fix),
                vec(RWKV_WIDTH), pl.BlockSpec((LANES, RWKV_WIDTH), fix),
                pl.BlockSpec((LANES, RWKV_WIDTH), fix),
                vec(RWKV_WIDTH), vec(RWKV_WIDTH), vec(RWKV_WIDTH), vec(RWKV_WIDTH), vec(RWKV_WIDTH),
                pl.BlockSpec((RWKV_GROUP * RWKV_HEAD_DIM, RWKV_GROUP * RWKV_HEAD_DIM), fix),
                pl.BlockSpec((L, L), fix)]
    return pl.pallas_call(
        _rwkv_kernel,
        grid=(bsz, tiles),
        in_specs=in_specs,
        out_specs=pl.BlockSpec((L, RWKV_WIDTH), lambda bi, i: (bi * tiles + i, 0)),
        out_shape=jax.ShapeDtypeStruct((n, RWKV_WIDTH), F32),
        scratch_shapes=[pltpu.VMEM((SUBLANES + L, 3 * RWKV_WIDTH), F32),
                        pltpu.VMEM((SUBLANES + L, LANES), F32),
                        pltpu.VMEM((SUBLANES + L, LANES), F32),
                        pltpu.VMEM((RWKV_HEADS // RWKV_GROUP, RWKV_GROUP * RWKV_HEAD_DIM,
                                    RWKV_GROUP * RWKV_HEAD_DIM), F32),
                        pltpu.VMEM((L, RWKV_WIDTH), F32)],
        compiler_params=_cparams(("arbitrary", "arbitrary"), 48 << 20),
    )(p, p, p, *prm)


def _causal_conv(ext_ref, x, w_ref, b_ref, first, rows, taps, halo, win_ref=None):
    @pl.when(first)
    def _():
        ext_ref[0:halo, :] = jnp.zeros((halo, x.shape[1]), F32)

    ext_ref[halo:halo + rows, :] = x
    acc = x * w_ref[taps - 1:taps, :] + b_ref[...]
    for b in range(min(SUBLANES, taps)):
        n_a = (taps - 1 - b) // SUBLANES + 1
        span = SUBLANES * (n_a - 1)
        if win_ref is not None and b > 0:
            win_ref[0:span + rows, :] = ext_ref[halo - span - b:halo - b + rows, :]
        for a in range(n_a):
            lag = SUBLANES * a + b
            if lag == 0:
                continue
            if win_ref is None or b == 0:
                tap = ext_ref[halo - lag:halo - lag + rows, :]
            else:
                tap = win_ref[span - SUBLANES * a:span - SUBLANES * a + rows, :]
            acc = acc + tap * w_ref[taps - 1 - lag:taps - lag, :]
    ext_ref[0:halo, :] = x[rows - halo:rows, :]
    return acc


def _lru_kernel(xb_ref, gb_ref, cw_ref, cb_ref, wa_ref, ba_ref, wx_ref, bx_ref, lam_ref, o_ref,
                ext_ref, carry_ref):
    rows = LRU_TILE
    first = pl.program_id(1) == 0

    @pl.when(first)
    def _():
        carry_ref[...] = jnp.zeros(carry_ref.shape, F32)

    xc = _causal_conv(ext_ref, xb_ref[...], cw_ref, cb_ref, first, rows, LRU_CONV, SUBLANES)
    gate_r = _sigmoid(_dot(xc, wa_ref[...]) + ba_ref[...])
    gate_i = _sigmoid(_dot(xc, wx_ref[...]) + bx_ref[...])
    log_a = -LRU_C * _softplus(-lam_ref[...]) * gate_r
    a = jnp.exp(log_a)
    u = jnp.sqrt(1.0 - jnp.exp(2.0 * log_a)) * (gate_i * xc)
    ridx = lax.broadcasted_iota(jnp.int32, (rows, LRU_WIDTH), 0)
    s = 1
    while s < SUBLANES:
        keep = ridx >= s
        a_sh = jnp.where(keep, pltpu.roll(a, s, axis=0), 1.0)
        u_sh = jnp.where(keep, pltpu.roll(u, s, axis=0), 0.0)
        u = a * u_sh + u
        a = a * a_sh
        s *= 2
    while s < rows:
        u = jnp.concatenate([u[0:s], a[s:rows] * u[0:rows - s] + u[s:rows]], axis=0)
        a = jnp.concatenate([a[0:s], a[s:rows] * a[0:rows - s]], axis=0)
        s *= 2
    h = a * carry_ref[0:1, :] + u
    carry_ref[...] = jnp.broadcast_to(h[rows - 1:rows, :], carry_ref.shape)
    o_ref[...] = h * _gelu_tanh(gb_ref[...])


def _lru(p, prm, bsz, seq):
    rows = LRU_TILE
    tiles = seq // rows
    n = bsz * seq
    row = lambda c: (lambda bi, i: (bi * tiles + i, c))
    fix = lambda bi, i: (0, 0)
    vec = pl.BlockSpec((1, LRU_WIDTH), fix)
    mat = pl.BlockSpec((LRU_WIDTH, LRU_WIDTH), fix)
    return pl.pallas_call(
        _lru_kernel,
        grid=(bsz, tiles),
        in_specs=[pl.BlockSpec((rows, LRU_WIDTH), row(3)), pl.BlockSpec((rows, LRU_WIDTH), row(4)),
                  pl.BlockSpec((LRU_CONV, LRU_WIDTH), fix), vec, mat, vec, mat, vec, vec],
        out_specs=pl.BlockSpec((rows, LRU_WIDTH), lambda bi, i: (bi * tiles + i, 0)),
        out_shape=jax.ShapeDtypeStruct((n, LRU_WIDTH), F32),
        scratch_shapes=[pltpu.VMEM((SUBLANES + rows, LRU_WIDTH), F32),
                        pltpu.VMEM((SUBLANES, LRU_WIDTH), F32)],
        compiler_params=_cparams(("arbitrary", "arbitrary"), 32 << 20),
    )(p, p, *prm)


def _conf_kernel(c_ref, cw_ref, cb_ref, g_ref, b_ref, o_ref, ext_ref, win_ref):
    rows = CONF_TILE
    first = pl.program_id(1) == 0
    c = c_ref[...]
    glu = c[:, 0:CONF_WIDTH] * _sigmoid(c[:, CONF_WIDTH:2 * CONF_WIDTH])
    u = _causal_conv(ext_ref, glu, cw_ref, cb_ref, first, rows, CONF_CONV, CONF_HALO, win_ref)
    mu = jnp.mean(u, axis=-1, keepdims=True)
    uc = u - mu
    var = jnp.mean(uc * uc, axis=-1, keepdims=True)
    o_ref[...] = _silu(uc * lax.rsqrt(var + LN_EPS) * g_ref[...] + b_ref[...])


def _conf(p, prm, bsz, seq):
    rows = CONF_TILE
    tiles = seq // rows
    n = bsz * seq
    fix = lambda bi, i: (0, 0)
    vec = pl.BlockSpec((1, CONF_WIDTH), fix)
    return pl.pallas_call(
        _conf_kernel,
        grid=(bsz, tiles),
        in_specs=[pl.BlockSpec((rows, 2 * CONF_WIDTH), lambda bi, i: (bi * tiles + i, 0)),
                  pl.BlockSpec((CONF_CONV, CONF_WIDTH), fix), vec, vec, vec],
        out_specs=pl.BlockSpec((rows, CONF_WIDTH), lambda bi, i: (bi * tiles + i, 0)),
        out_shape=jax.ShapeDtypeStruct((n, CONF_WIDTH), F32),
        scratch_shapes=[pltpu.VMEM((CONF_HALO + rows, CONF_WIDTH), F32),
                        pltpu.VMEM((CONF_HALO + rows, CONF_WIDTH), F32)],
        compiler_params=_cparams(("arbitrary", "arbitrary"), 32 << 20),
    )(p, *prm)


def _ssd_kernel(z_ref, xs_ref, bc_ref, dt_ref, cwx_ref, cbx_ref, cwb_ref, cbb_ref, dtb_ref, alog_ref,
                dskip_ref, normg_ref, expand_ref, tril_ref, o_ref, ext_xs, ext_bc, state_ref, ybuf_ref):
    L = SSM_CHUNK
    P = SSM_HEAD_DIM
    NS = SSM_STATE
    HG = SSM_HEADS // SSM_GROUPS
    first = pl.program_id(1) == 0

    @pl.when(first)
    def _():
        state_ref[...] = jnp.zeros(state_ref.shape, F32)

    xs = _silu(_causal_conv(ext_xs, xs_ref[...], cwx_ref, cbx_ref, first, L, SSM_CONV, SUBLANES))
    bc = _silu(_causal_conv(ext_bc, bc_ref[...], cwb_ref, cbb_ref, first, L, SSM_CONV, SUBLANES))
    dt = _softplus(dt_ref[...] + dtb_ref[...])
    a_head = -jnp.exp(alog_ref[...])
    acum = _dot_onehot_lhs(tril_ref[...], dt * a_head, 3)
    acum_t = acum.T
    acum_x = _dot_onehot_rhs(acum, expand_ref[...], 3)
    acum_xl = acum_x[L - 1:L, :]
    xdt = xs * _dot_onehot_rhs(dt, expand_ref[...], 3)
    causal = lax.broadcasted_iota(jnp.int32, (L, L), 1) <= lax.broadcasted_iota(jnp.int32, (L, L), 0)

    half = SSM_INNER // SSM_GROUPS
    x_end = (xdt * jnp.exp(acum_xl - acum_x)).astype(BF16)
    state = state_ref[...]
    cb = []
    for g in range(SSM_GROUPS):
        gs = slice(g * half, (g + 1) * half)
        bm = bc[:, g * NS:(g + 1) * NS]
        cm = bc[:, (SSM_GROUPS + g) * NS:(SSM_GROUPS + g + 1) * NS].astype(BF16)
        cb.append(_dot_nt(cm, bm))
        ybuf_ref[:, gs] = _dot(cm, state[:, gs])
        state_ref[:, gs] = state[:, gs] * jnp.exp(acum_xl[:, gs]) + _dot(bm.T, x_end[:, gs])
    y_off = ybuf_ref[...] * jnp.exp(acum_x)
    for h0 in range(0, SSM_HEADS, SSM_HEAD_BATCH):
        hs = range(h0, h0 + SSM_HEAD_BATCH)
        decay = {h: jnp.exp(jnp.where(causal, acum[:, h:h + 1] - acum_t[h:h + 1, :], -jnp.inf)) for h in hs}
        y_diag = {h: _dot(cb[h // HG] * decay[h], xdt[:, h * P:(h + 1) * P]) for h in hs}
        for h in hs:
            ybuf_ref[:, h * P:(h + 1) * P] = y_diag[h]

    y = (ybuf_ref[...] + y_off + dskip_ref[...] * xs) * _silu(z_ref[...])
    for g in range(SSM_GROUPS):
        yg = y[:, g * half:(g + 1) * half]
        ms = jnp.mean(yg * yg, axis=-1, keepdims=True)
        o_ref[:, g * half:(g + 1) * half] = yg * lax.rsqrt(ms + LN_EPS) * normg_ref[:, g * half:(g + 1) * half]


def _ssd(p, prm, bsz, seq):
    L = SSM_CHUNK
    tiles = seq // L
    n = bsz * seq
    row = lambda c: (lambda bi, i: (bi * tiles + i, c))
    fix = lambda bi, i: (0, 0)
    bcw = 2 * SSM_GROUPS * SSM_STATE
    vec = lambda width: pl.BlockSpec((1, width), fix)
    in_specs = [pl.BlockSpec((L, SSM_INNER), row(1)), pl.BlockSpec((L, SSM_INNER), row(2)),
                pl.BlockSpec((L, bcw), row(6)), pl.BlockSpec((L, LANES), row(28)),
                pl.BlockSpec((SSM_CONV, SSM_INNER), fix), vec(SSM_INNER),
                pl.BlockSpec((SSM_CONV, bcw), fix), vec(bcw),
                vec(LANES), vec(LANES), vec(SSM_INNER), vec(SSM_INNER),
                pl.BlockSpec((LANES, SSM_INNER), fix), pl.BlockSpec((L, L), fix)]
    return pl.pallas_call(
        _ssd_kernel,
        grid=(bsz, tiles),
        in_specs=in_specs,
        out_specs=pl.BlockSpec((L, SSM_INNER), lambda bi, i: (bi * tiles + i, 0)),
        out_shape=jax.ShapeDtypeStruct((n, SSM_INNER), F32),
        scratch_shapes=[pltpu.VMEM((SUBLANES + L, SSM_INNER), F32),
                        pltpu.VMEM((SUBLANES + L, bcw), F32),
                        pltpu.VMEM((SSM_STATE, SSM_INNER), F32),
                        pltpu.VMEM((L, SSM_INNER), F32)],
        compiler_params=_cparams(("arbitrary", "arbitrary"), 32 << 20),
    )(p, p, p, p, *prm)


def _row(v):
    return v.reshape(1, -1).astype(F32)


def _pad_rows(m, rows, offset):
    out = jnp.zeros((rows, m.shape[1]), F32)
    return out.at[offset:offset + m.shape[0]].set(m)


def _block_diag(w):
    nb, d, e = w.shape
    eye = jnp.eye(nb, dtype=w.dtype)
    return (eye[:, None, :, None] * w[:, :, None, :]).reshape(nb * d, nb * e)


def _tril_ones(n, blocks=1):
    return jnp.asarray(np.kron(np.eye(blocks, dtype=np.float32), np.tril(np.ones((n, n), np.float32))), dtype=BF16)


def _head_sum_matrix():
    idx = np.arange(RWKV_GROUP * RWKV_HEAD_DIM) // RWKV_HEAD_DIM
    return jnp.asarray((idx[:, None] == idx[None, :]).astype(np.float32), dtype=BF16)


def _head_expand_matrix():
    m = np.zeros((LANES, SSM_INNER), np.float32)
    for h in range(SSM_HEADS):
        m[h, h * SSM_HEAD_DIM:(h + 1) * SSM_HEAD_DIM] = 1.0
    return jnp.asarray(m, dtype=BF16)


def _sublayers(h, mem2, i, bsz, xa_wq, xa_wk, xa_wv, xa_wo, ffn_w1, ffn_w2, ln_mem_g, ln_mem_b, ln_ffn_g, ln_ffn_b):
    kmem = _proj(mem2, xa_wk[i].astype(BF16), mem2.shape[0] // bsz, BF16)
    vmem_ = _proj(mem2, xa_wv[i].astype(BF16), mem2.shape[0] // bsz, BF16)
    h = _attn(h, kmem, vmem_, xa_wq[i].astype(BF16), xa_wo[i].astype(BF16), _row(ln_mem_g[i]), _row(ln_mem_b[i]),
              bsz, 512)
    return _ffn(h, ffn_w1[i].astype(BF16), ffn_w2[i].astype(BF16), _row(ln_ffn_g[i]), _row(ln_ffn_b[i]), 512)


def kernel(x, mem, ev_w_in, ev_mu, ev_w0, ev_w2, ev_a0, ev_a2, ev_g2, ev_k_k, ev_k_a, ev_r_k, ev_gn_g, ev_gn_b, ev_lru_conv_w, ev_lru_conv_b, ev_lru_wa, ev_lru_ba, ev_lru_wx, ev_lru_bx, ev_lru_lam, ev_w_out, od_w_in, od_cf_conv_w, od_cf_conv_b, od_cf_ln_g, od_cf_ln_b, od_ssm_conv_w, od_ssm_conv_b, od_dt_bias, od_a_log, od_d_skip, od_ssm_norm_g, od_w_out, xa_wq, xa_wk, xa_wv, xa_wo, ffn_w1, ffn_w2, ln_mix_g, ln_mix_b, ln_mem_g, ln_mem_b, ln_ffn_g, ln_ffn_b):
    bsz, seq, _ = x.shape
    n = bsz * seq
    h = x.reshape(n, D_MODEL)
    mem2 = mem.reshape(bsz * mem.shape[1], D_MODEL)
    xa = (xa_wq, xa_wk, xa_wv, xa_wo, ffn_w1, ffn_w2, ln_mem_g, ln_mem_b, ln_ffn_g, ln_ffn_b)

    rw = 3 * RWKV_WIDTH
    lo = 2 * LANES
    w_in = jnp.concatenate([ev_w_in[0][:, 0:rw], ev_w_in[0][:, rw + lo:], ev_w_in[0][:, rw:rw + lo]], axis=1)
    p = _proj(h, w_in.astype(BF16), 512)
    mu = ev_mu[0]
    rwkv_prm = (_row(mu[0:rw]), _row(mu[rw:rw + LANES]), _row(mu[rw + LANES:rw + lo]),
                _row(ev_w0[0]), _pad_rows(ev_w2[0], LANES, 0), _row(ev_a0[0]), _pad_rows(ev_a2[0], LANES, 64),
                ev_g2[0].astype(F32), _row(ev_k_k[0]), _row(ev_k_a[0]), _row(ev_r_k[0]), _row(ev_gn_g[0]),
                _row(ev_gn_b[0]), _head_sum_matrix(), _tril_ones(RWKV_CHUNK, RWKV_STEP_CHUNKS))
    y_a = _rwkv(p, rwkv_prm, bsz, seq)
    lru_prm = (ev_lru_conv_w[0], _row(ev_lru_conv_b[0]), _block_diag(ev_lru_wa[0]).astype(BF16), _row(ev_lru_ba[0]),
               _block_diag(ev_lru_wx[0]).astype(BF16), _row(ev_lru_bx[0]), _row(ev_lru_lam[0]))
    y_b = _lru(p, lru_prm, bsz, seq)
    w_out = ev_w_out[0].astype(BF16)
    h = _mix_out(y_a, y_b, w_out[0:RWKV_WIDTH], w_out[RWKV_WIDTH:], h, _row(ln_mix_g[0]), _row(ln_mix_b[0]), 512)
    h = _sublayers(h, mem2, 0, bsz, *xa)

    w_in = jnp.pad(od_w_in[0], ((0, 0), (0, ODD_IN_PAD - ODD_IN))).astype(BF16)
    p = _proj(h, w_in, 256)
    conf_prm = (od_cf_conv_w[0], _row(od_cf_conv_b[0]), _row(od_cf_ln_g[0]), _row(od_cf_ln_b[0]))
    y_c = _conf(p, conf_prm, bsz, seq)
    cw, cb = od_ssm_conv_w[0], od_ssm_conv_b[0]
    pad16 = lambda v: jnp.pad(v, (0, LANES - SSM_HEADS)).reshape(1, LANES).astype(F32)
    ssd_prm = (cw[:, 0:SSM_INNER], _row(cb[0:SSM_INNER]), cw[:, SSM_INNER:], _row(cb[SSM_INNER:]),
               pad16(od_dt_bias[0]), pad16(od_a_log[0]), _row(jnp.repeat(od_d_skip[0], SSM_HEAD_DIM)),
               _row(od_ssm_norm_g[0]), _head_expand_matrix(), _tril_ones(SSM_CHUNK))
    y_d = _ssd(p, ssd_prm, bsz, seq)
    w_out = od_w_out[0].astype(BF16)
    h = _mix_out(y_c, y_d, w_out[0:CONF_WIDTH], w_out[CONF_WIDTH:], h, _row(ln_mix_g[1]), _row(ln_mix_b[1]), 512)
    h = _sublayers(h, mem2, 1, bsz, *xa)
    return h.reshape(bsz, seq, D_MODEL)
```

```python
import jax
import jax.numpy as jnp
import numpy as np
from jax import lax
from jax.experimental import pallas as pl
from jax.experimental.pallas import tpu as pltpu

F32 = jnp.float32
BF16 = jnp.bfloat16

D_MODEL = 1024
DEPTH = 2
DN_ALPHA = (2 * DEPTH) ** 0.25
LN_EPS = 1e-5

RWKV_WIDTH = 512
RWKV_HEAD_DIM = 64
RWKV_HEADS = 8
RWKV_GN_EPS = 64e-5
RWKV_CHUNK = 64
RWKV_STEP_CHUNKS = 4
RWKV_GROUP = 4

LRU_WIDTH = 512
LRU_C = 8.0
LRU_CONV = 4
LRU_TILE = 256

CONF_WIDTH = 512
CONF_CONV = 31
CONF_HALO = 32
CONF_TILE = 256

SSM_INNER = 1024
SSM_HEAD_DIM = 64
SSM_HEADS = 16
SSM_GROUPS = 2
SSM_STATE = 128
SSM_CONV = 4
SSM_CHUNK = 128
SSM_HEAD_BATCH = 4
ODD_IN = 3600
ODD_IN_PAD = 3712

EPILOGUE_SPLIT = 2
XA_HEADS = 4
XA_HEAD_DIM = 256
D_FF = 4096

SUBLANES = 8
LANES = 128
VMEM_CAP = 56 * 1024 * 1024


def _cparams(semantics, vmem_bytes):
    return pltpu.CompilerParams(dimension_semantics=semantics,
                                vmem_limit_bytes=int(min(max(vmem_bytes, 16 * 1024 * 1024), VMEM_CAP)))


def _dot(a, b):
    return jnp.dot(a.astype(BF16), b.astype(BF16), preferred_element_type=F32)


def _split_parts(x, parts):
    out = []
    for _ in range(parts - 1):
        hi = x.astype(BF16)
        out.append(hi)
        x = x - hi.astype(F32)
    out.append(x.astype(BF16))
    return out


def _dot_onehot_rhs(x, m, parts):
    acc = None
    for xp in _split_parts(x, parts):
        t = jnp.dot(xp, m, preferred_element_type=F32)
        acc = t if acc is None else acc + t
    return acc


def _dot_onehot_lhs(m, x, parts):
    acc = None
    for xp in _split_parts(x, parts):
        t = jnp.dot(m, xp, preferred_element_type=F32)
        acc = t if acc is None else acc + t
    return acc


def _head_sums(terms, bdm):
    pieces, owner = [], []
    for n, (x, parts) in enumerate(terms):
        for p in _split_parts(x, parts):
            pieces.append(p)
            owner.append(n)
    rows = terms[0][0].shape[0]
    w = bdm.shape[0]
    stacked = jnp.concatenate(pieces, axis=0)
    sums = jnp.concatenate([jnp.dot(stacked[:, q * w:(q + 1) * w], bdm, preferred_element_type=F32)
                            for q in range(stacked.shape[1] // w)], axis=1)
    out = [None] * len(terms)
    for j, n in enumerate(owner):
        piece = sums[j * rows:(j + 1) * rows]
        out[n] = piece if out[n] is None else out[n] + piece
    return out


def _dot_nt(a, b):
    return lax.dot_general(a.astype(BF16), b.astype(BF16), (((1,), (1,)), ((), ())),
                           preferred_element_type=F32)


def _sigmoid(x):
    return 1.0 / (1.0 + jnp.exp(-x))


def _softplus(x):
    return jnp.maximum(x, 0.0) + jnp.log(1.0 + jnp.exp(-jnp.abs(x)))


def _silu(x):
    return x * _sigmoid(x)


def _gelu_tanh(x):
    return 0.5 * x * (1.0 + jnp.tanh(0.7978845608028654 * (x + 0.044715 * (x * x * x))))


def _res_ln(h, y, g, b):
    z = DN_ALPHA * h + y
    mu = jnp.mean(z, axis=-1, keepdims=True)
    zc = z - mu
    var = jnp.mean(zc * zc, axis=-1, keepdims=True)
    return zc * lax.rsqrt(var + LN_EPS) * g + b


def _proj_kernel(x_ref, w_ref, o_ref):
    o_ref[...] = _dot(x_ref[...], w_ref[...]).astype(o_ref.dtype)


def _proj(x, w, tm, out_dtype=F32):
    n, k = x.shape
    m = w.shape[1]
    vmem = 2 * (tm * k * x.dtype.itemsize + k * m * 2 + tm * m * jnp.dtype(out_dtype).itemsize) + (4 << 20)
    return pl.pallas_call(
        _proj_kernel,
        grid=(n // tm,),
        in_specs=[pl.BlockSpec((tm, k), lambda i: (i, 0)), pl.BlockSpec((k, m), lambda i: (0, 0))],
        out_specs=pl.BlockSpec((tm, m), lambda i: (i, 0)),
        out_shape=jax.ShapeDtypeStruct((n, m), out_dtype),
        compiler_params=_cparams(("parallel",), vmem),
    )(x, w)


def _mix_out_kernel(a_ref, b_ref, wa_ref, wb_ref, h_ref, g_ref, bias_ref, o_ref):
    rows = o_ref.shape[0] // EPILOGUE_SPLIT
    ys = []
    for s in range(EPILOGUE_SPLIT):
        sl = slice(s * rows, (s + 1) * rows)
        ys.append(_dot(a_ref[sl, :], wa_ref[...]) + _dot(b_ref[sl, :], wb_ref[...]))
    for s in range(EPILOGUE_SPLIT):
        sl = slice(s * rows, (s + 1) * rows)
        o_ref[sl, :] = _res_ln(h_ref[sl, :], ys[s], g_ref[...], bias_ref[...])


def _mix_out(a, b, wa, wb, h, g, bias, tm):
    n = h.shape[0]
    ka, kb = a.shape[1], b.shape[1]
    vmem = 2 * (tm * (ka + kb) * 2 + 2 * tm * D_MODEL * 4 + (ka + kb) * D_MODEL * 2) + (8 << 20)
    row = lambda i: (i, 0)
    fix = lambda i: (0, 0)
    return pl.pallas_call(
        _mix_out_kernel,
        grid=(n // tm,),
        in_specs=[pl.BlockSpec((tm, ka), row), pl.BlockSpec((tm, kb), row),
                  pl.BlockSpec((ka, D_MODEL), fix), pl.BlockSpec((kb, D_MODEL), fix),
                  pl.BlockSpec((tm, D_MODEL), row), pl.BlockSpec((1, D_MODEL), fix),
                  pl.BlockSpec((1, D_MODEL), fix)],
        out_specs=pl.BlockSpec((tm, D_MODEL), row),
        out_shape=jax.ShapeDtypeStruct((n, D_MODEL), F32),
        compiler_params=_cparams(("parallel",), vmem),
    )(a, b, wa, wb, h, g, bias)


def _attn_kernel(h_ref, k_ref, v_ref, wq_ref, wo_ref, g_ref, b_ref, o_ref, cat_ref):
    rows = o_ref.shape[0] // EPILOGUE_SPLIT
    subs = [slice(s * rows, (s + 1) * rows) for s in range(EPILOGUE_SPLIT)]
    q = [_dot(h_ref[rs, :], wq_ref[...]) for rs in subs]
    for hd in range(XA_HEADS):
        sl = slice(hd * XA_HEAD_DIM, (hd + 1) * XA_HEAD_DIM)
        for n, rs in enumerate(subs):
            s = _dot_nt(q[n][:, sl], k_ref[:, sl]) * (XA_HEAD_DIM ** -0.5)
            s = s - jnp.max(s, axis=-1, keepdims=True)
            e = jnp.exp(s)
            p = e / jnp.sum(e, axis=-1, keepdims=True)
            cat_ref[rs, sl] = _dot(p, v_ref[:, sl])
    y = [_dot(cat_ref[rs, :], wo_ref[...]) for rs in subs]
    for n, rs in enumerate(subs):
        o_ref[rs, :] = _res_ln(h_ref[rs, :], y[n], g_ref[...], b_ref[...])


def _attn(h, kmem, vmem_, wq, wo, g, b, bsz, tm):
    n = h.shape[0]
    tiles = n // bsz // tm
    n_mem = kmem.shape[0] // bsz
    vmem = 2 * (2 * tm * D_MODEL * 4 + 2 * n_mem * D_MODEL * 2 + 2 * D_MODEL * D_MODEL * 2) + tm * D_MODEL * 16 + (8 << 20)
    row = lambda bi, i: (bi * tiles + i, 0)
    mem = lambda bi, i: (bi, 0)
    fix = lambda bi, i: (0, 0)
    return pl.pallas_call(
        _attn_kernel,
        grid=(bsz, tiles),
        in_specs=[pl.BlockSpec((tm, D_MODEL), row), pl.BlockSpec((n_mem, D_MODEL), mem),
                  pl.BlockSpec((n_mem, D_MODEL), mem), pl.BlockSpec((D_MODEL, D_MODEL), fix),
                  pl.BlockSpec((D_MODEL, D_MODEL), fix), pl.BlockSpec((1, D_MODEL), fix),
                  pl.BlockSpec((1, D_MODEL), fix)],
        out_specs=pl.BlockSpec((tm, D_MODEL), row),
        out_shape=jax.ShapeDtypeStruct((n, D_MODEL), F32),
        scratch_shapes=[pltpu.VMEM((tm, D_MODEL), F32)],
        compiler_params=_cparams(("parallel", "parallel"), vmem),
    )(h, kmem, vmem_, wq, wo, g, b)


def _ffn_kernel(h_ref, w1_ref, w2_ref, g_ref, b_ref, o_ref):
    rows = o_ref.shape[0] // EPILOGUE_SPLIT
    subs = [slice(s * rows, (s + 1) * rows) for s in range(EPILOGUE_SPLIT)]
    u = [jnp.square(jnp.maximum(_dot(h_ref[rs, :], w1_ref[...]), 0.0)).astype(BF16) for rs in subs]
    y = [_dot(u[n], w2_ref[...]) for n in range(EPILOGUE_SPLIT)]
    for n, rs in enumerate(subs):
        o_ref[rs, :] = _res_ln(h_ref[rs, :], y[n], g_ref[...], b_ref[...])


def _ffn(h, w1, w2, g, b, tm):
    n = h.shape[0]
    vmem = 2 * D_MODEL * D_FF * 2 + 4 * tm * D_MODEL * 4 + tm * D_FF * 6 + tm * D_MODEL * 6 + (6 << 20)
    fix = lambda i: (0, 0)
    return pl.pallas_call(
        _ffn_kernel,
        grid=(n // tm,),
        in_specs=[pl.BlockSpec((tm, D_MODEL), lambda i: (i, 0)),
                  pl.BlockSpec((D_MODEL, D_FF), fix, pipeline_mode=pl.Buffered(1)),
                  pl.BlockSpec((D_FF, D_MODEL), fix, pipeline_mode=pl.Buffered(1)),
                  pl.BlockSpec((1, D_MODEL), fix),
                  pl.BlockSpec((1, D_MODEL), fix)],
        out_specs=pl.BlockSpec((tm, D_MODEL), lambda i: (i, 0)),
        out_shape=jax.ShapeDtypeStruct((n, D_MODEL), F32),
        compiler_params=_cparams(("parallel",), vmem),
    )(h, w1, w2, g, b)


def _shifted(ext_ref, x, first, rows):
    if first is not None:
        @pl.when(first)
        def _():
            ext_ref[0:SUBLANES, :] = jnp.zeros((SUBLANES, x.shape[1]), F32)

    ext_ref[SUBLANES:SUBLANES + rows, :] = x
    prev = ext_ref[SUBLANES - 1:SUBLANES - 1 + rows, :]
    ext_ref[0:SUBLANES, :] = x[rows - SUBLANES:rows, :]
    return prev


def _causal_conv(ext_ref, x, w_ref, b_ref, first, rows, taps, halo, win_ref=None):
    if first is not None:
        @pl.when(first)
        def _():
            ext_ref[0:halo, :] = jnp.zeros((halo, x.shape[1]), F32)

    ext_ref[halo:halo + rows, :] = x
    acc = x * w_ref[taps - 1:taps, :] + b_ref[...]
    for b in range(min(SUBLANES, taps)):
        n_a = (taps - 1 - b) // SUBLANES + 1
        span = SUBLANES * (n_a - 1)
        if win_ref is not None and b > 0:
            win_ref[0:span + rows, :] = ext_ref[halo - span - b:halo - b + rows, :]
        for a in range(n_a):
            lag = SUBLANES * a + b
            if lag == 0:
                continue
            if win_ref is None or b == 0:
                tap = ext_ref[halo - lag:halo - lag + rows, :]
            else:
                tap = win_ref[span - SUBLANES * a:span - SUBLANES * a + rows, :]
            acc = acc + tap * w_ref[taps - 1 - lag:taps - lag, :]
    ext_ref[0:halo, :] = x[rows - halo:rows, :]
    return acc


def _lru_stages(xb_ref, gb_ref, cw_ref, cb_ref, wa_ref, ba_ref, wx_ref, bx_ref, lam_ref, o_ref, ext_ref, carry_ref):
    rows = LRU_TILE
    xc = _causal_conv(ext_ref, xb_ref[...], cw_ref, cb_ref, None, rows, LRU_CONV, SUBLANES)
    yield
    gate_r = _sigmoid(_dot(xc, wa_ref[...]) + ba_ref[...])
    gate_i = _sigmoid(_dot(xc, wx_ref[...]) + bx_ref[...])
    yield
    log_a = -LRU_C * _softplus(-lam_ref[...]) * gate_r
    a = jnp.exp(log_a)
    u = jnp.sqrt(1.0 - jnp.exp(2.0 * log_a)) * (gate_i * xc)
    yield
    ridx = lax.broadcasted_iota(jnp.int32, (rows, LRU_WIDTH), 0)
    s = 1
    while s < SUBLANES:
        keep = ridx >= s
        a_sh = jnp.where(keep, pltpu.roll(a, s, axis=0), 1.0)
        u_sh = jnp.where(keep, pltpu.roll(u, s, axis=0), 0.0)
        u = a * u_sh + u
        a = a * a_sh
        s *= 2
        yield
    while s < rows:
        u = jnp.concatenate([u[0:s], a[s:rows] * u[0:rows - s] + u[s:rows]], axis=0)
        a = jnp.concatenate([a[0:s], a[s:rows] * a[0:rows - s]], axis=0)
        s *= 2
        yield
    h = a * carry_ref[0:1, :] + u
    carry_ref[...] = jnp.broadcast_to(h[rows - 1:rows, :], carry_ref.shape)
    o_ref[...] = (h * _gelu_tanh(gb_ref[...])).astype(o_ref.dtype)


def _mix0_kernel(rkv_ref, wa_ref, gd_ref, mu_rkv_ref, mu_wa_ref, mu_gd_ref, w0_ref, w2_ref, a0_ref, a2_ref,
                 g2_ref, kk_ref, ka_ref, rk_ref, gng_ref, gnb_ref, bdm_ref, tril_ref,
                 xb_ref, gb_ref, cw_ref, cb_ref, lwa_ref, lba_ref, lwx_ref, lbx_ref, lam_ref,
                 o_ref, o_lru_ref,
                 ext_rkv, ext_wa, ext_gd, state_ref, obuf_ref, ext_lru, carry_lru):
    L = RWKV_CHUNK
    CH = RWKV_STEP_CHUNKS
    rows = L * CH
    N = RWKV_HEAD_DIM

    @pl.when(pl.program_id(1) == 0)
    def _():
        state_ref[...] = jnp.zeros(state_ref.shape, F32)
        carry_lru[...] = jnp.zeros(carry_lru.shape, F32)
        for ext in (ext_rkv, ext_wa, ext_gd, ext_lru):
            ext[0:SUBLANES, :] = jnp.zeros((SUBLANES, ext.shape[1]), F32)

    lru = _lru_stages(xb_ref, gb_ref, cw_ref, cb_ref, lwa_ref, lba_ref, lwx_ref, lbx_ref, lam_ref, o_lru_ref,
                      ext_lru, carry_lru)
    tick = lambda: next(lru, None)

    x = rkv_ref[...]
    xm = x + (_shifted(ext_rkv, x, None, rows) - x) * mu_rkv_ref[...]
    wa = wa_ref[...]
    wam = wa + (_shifted(ext_wa, wa, None, rows) - wa) * mu_wa_ref[...]
    gd = gd_ref[...]
    gdm = gd + (_shifted(ext_gd, gd, None, rows) - gd) * mu_gd_ref[...]

    r = xm[:, 0:RWKV_WIDTH]
    k = xm[:, RWKV_WIDTH:2 * RWKV_WIDTH]
    v = xm[:, 2 * RWKV_WIDTH:3 * RWKV_WIDTH]
    w = -_softplus(-(w0_ref[...] + _dot(jnp.tanh(wam), w2_ref[...]))) - 0.5
    logdec = -jnp.exp(w)
    a = _sigmoid(a0_ref[...] + _dot(wam, a2_ref[...]))
    g = _dot(_sigmoid(gdm), g2_ref[...])
    bdm = bdm_ref[...]
    kks = k * kk_ref[...]
    k2 = k * (1.0 + (a - 1.0) * ka_ref[...])
    kk_sq, bonus_s = _head_sums([(kks * kks, 2), (r * k2 * rk_ref[...], 1)], bdm)
    kk = kks / jnp.maximum(jnp.sqrt(kk_sq), 1e-12)
    bonus = bonus_s * v
    beta = kk * a

    cum = _dot_onehot_lhs(tril_ref[...], logdec, 2)
    cum_l = jnp.concatenate(
        [jnp.broadcast_to(cum[c * L + L - 1:c * L + L, :], (L, RWKV_WIDTH)) for c in range(CH)], axis=0)
    e_to_end = jnp.exp(cum_l - cum)
    e_neg = jnp.exp(-cum)
    a_t = -kk * jnp.exp(cum - logdec)
    r_t = r * jnp.exp(cum)
    b_t = beta * e_neg
    k_t = k2 * e_neg
    b_p = beta * e_to_end
    k_p = k2 * e_to_end
    p_l = jnp.exp(cum_l)

    W = RWKV_GROUP * N
    bdm32 = bdm.astype(F32)
    lane_s = jnp.bitwise_and(lax.broadcasted_iota(jnp.int32, (L, W), 1), N - 1)
    t_idx = lax.broadcasted_iota(jnp.int32, (L, W), 0)
    strict = lane_s < t_idx
    incl = lane_s <= t_idx
    eye_cat = jnp.where(lane_s == t_idx, 1.0, 0.0)
    mask2 = jnp.concatenate([strict, incl], axis=0)

    def bd(y):
        return jnp.concatenate([y.astype(BF16)] * RWKV_GROUP, axis=0) * bdm

    mmb = lambda x, y: jnp.dot(x.astype(BF16), y, preferred_element_type=F32)
    mmb_nt = lambda x, y: lax.dot_general(x.astype(BF16), y, (((1,), (1,)), ((), ())), preferred_element_type=F32)

    items = [(c, q) for c in range(CH) for q in range(RWKV_WIDTH // W)]
    cut = lambda arr, c, q: arr[c * L:(c + 1) * L, q * W:(q + 1) * W]
    a_c = {i: cut(a_t, *i) for i in items}
    r_c = {i: cut(r_t, *i) for i in items}
    v_c = {i: cut(v, *i) for i in items}
    bp_c = {i: cut(b_p, *i) for i in items}
    lhs = {i: jnp.concatenate([a_c[i], r_c[i]], axis=0).astype(BF16) for i in items}
    q_b = {i: mmb_nt(lhs[i], bd(cut(b_t, *i))) for i in items}
    tick()
    q_k = {i: jnp.where(mask2, mmb_nt(lhs[i], bd(cut(k_t, *i))), 0.0) for i in items}
    tick()
    qkv = {i: mmb(q_k[i], bd(v_c[i])) for i in items}
    tick()
    pw = {i: jnp.where(strict, q_b[i][0:L], 0.0) for i in items}
    tinv = {i: eye_cat + pw[i] for i in items}
    pw = {i: mmb(pw[i], bd(pw[i])) for i in items}
    tick()
    for _ in range(L.bit_length() - 3):
        both = {i: mmb(jnp.concatenate([pw[i], tinv[i]], axis=0), bd(pw[i])) for i in items}
        pw = {i: both[i][0:L] for i in items}
        tinv = {i: tinv[i] + both[i][L:2 * L] for i in items}
        tick()
    tinv = {i: tinv[i] + mmb(tinv[i], bd(pw[i])) for i in items}
    tick()
    ta = {i: mmb(tinv[i], bd(a_c[i])) for i in items}
    w2 = {i: mmb(tinv[i], bd(qkv[i][0:L])) for i in items}
    tick()
    a_rb = {i: jnp.where(incl, q_b[i][L:2 * L], 0.0).astype(BF16) for i in items}
    q_t = {i: r_c[i] + mmb(a_rb[i], bd(ta[i])) for i in items}
    o_loc = {i: mmb(a_rb[i], bd(w2[i])) + qkv[i][L:2 * L] for i in items}
    tick()
    gk = {i: _dot(ta[i].T, bp_c[i]).astype(BF16) * bdm for i in items}
    s_loc = {i: _dot(jnp.concatenate([w2[i], v_c[i]], axis=0).T,
                     jnp.concatenate([bp_c[i], cut(k_p, *i)], axis=0)) * bdm32 for i in items}
    tick()
    state = [state_ref[q] for q in range(RWKV_WIDTH // W)]
    for c in range(CH):
        for q in range(RWKV_WIDTH // W):
            i = (c, q)
            obuf_ref[c * L:(c + 1) * L, q * W:(q + 1) * W] = mmb_nt(q_t[i], state[q].astype(BF16)) + o_loc[i]
            state[q] = state[q] * p_l[c * L:c * L + 1, q * W:(q + 1) * W] + mmb(state[q], gk[i]) + s_loc[i]
        tick()
    for q in range(RWKV_WIDTH // W):
        state_ref[q] = state[q]

    o = obuf_ref[...]
    oc = o - _head_sums([(o, 1)], bdm)[0] * (1.0 / N)
    var = _head_sums([(oc * oc, 1)], bdm)[0] * (1.0 / N)
    o_ref[...] = ((oc * lax.rsqrt(var + RWKV_GN_EPS) * gng_ref[...] + gnb_ref[...] + bonus) * g).astype(o_ref.dtype)
    for _ in lru:
        pass


def _mix0(p, rwkv_prm, lru_prm, bsz, seq):
    L = RWKV_CHUNK * RWKV_STEP_CHUNKS
    assert L == LRU_TILE
    tiles = seq // L
    n = bsz * seq
    gw = RWKV_GROUP * RWKV_HEAD_DIM
    row = lambda c: (lambda bi, i: (bi * tiles + i, c))
    fix = lambda bi, i: (0, 0)
    vec = lambda width: pl.BlockSpec((1, width), fix)
    mat = lambda r, c: pl.BlockSpec((r, c), fix)
    in_specs = [pl.BlockSpec((L, 3 * RWKV_WIDTH), row(0)),
                pl.BlockSpec((L, LANES), row(20)), pl.BlockSpec((L, LANES), row(21)),
                vec(3 * RWKV_WIDTH), vec(LANES), vec(LANES),
                vec(RWKV_WIDTH), mat(LANES, RWKV_WIDTH), vec(RWKV_WIDTH), mat(LANES, RWKV_WIDTH),
                mat(LANES, RWKV_WIDTH),
                vec(RWKV_WIDTH), vec(RWKV_WIDTH), vec(RWKV_WIDTH), vec(RWKV_WIDTH), vec(RWKV_WIDTH),
                mat(gw, gw), mat(L, L),
                pl.BlockSpec((L, LRU_WIDTH), row(3)), pl.BlockSpec((L, LRU_WIDTH), row(4)),
                mat(LRU_CONV, LRU_WIDTH), vec(LRU_WIDTH), mat(LRU_WIDTH, LRU_WIDTH), vec(LRU_WIDTH),
                mat(LRU_WIDTH, LRU_WIDTH), vec(LRU_WIDTH), vec(LRU_WIDTH)]
    out_row = lambda bi, i: (bi * tiles + i, 0)
    return pl.pallas_call(
        _mix0_kernel,
        grid=(bsz, tiles),
        in_specs=in_specs,
        out_specs=[pl.BlockSpec((L, RWKV_WIDTH), out_row), pl.BlockSpec((L, LRU_WIDTH), out_row)],
        out_shape=[jax.ShapeDtypeStruct((n, RWKV_WIDTH), BF16), jax.ShapeDtypeStruct((n, LRU_WIDTH), BF16)],
        scratch_shapes=[pltpu.VMEM((SUBLANES + L, 3 * RWKV_WIDTH), F32),
                        pltpu.VMEM((SUBLANES + L, LANES), F32),
                        pltpu.VMEM((SUBLANES + L, LANES), F32),
                        pltpu.VMEM((RWKV_HEADS // RWKV_GROUP, gw, gw), F32),
                        pltpu.VMEM((L, RWKV_WIDTH), F32),
                        pltpu.VMEM((SUBLANES + L, LRU_WIDTH), F32),
                        pltpu.VMEM((SUBLANES, LRU_WIDTH), F32)],
        compiler_params=_cparams(("arbitrary", "arbitrary"), 48 << 20),
    )(p, p, p, *rwkv_prm, p, p, *lru_prm)


def _conf_kernel(c_ref, cw_ref, cb_ref, g_ref, b_ref, o_ref, ext_ref, win_ref):
    rows = CONF_TILE
    first = pl.program_id(1) == 0
    c = c_ref[...]
    glu = c[:, 0:CONF_WIDTH] * _sigmoid(c[:, CONF_WIDTH:2 * CONF_WIDTH])
    u = _causal_conv(ext_ref, glu, cw_ref, cb_ref, first, rows, CONF_CONV, CONF_HALO, win_ref)
    mu = jnp.mean(u, axis=-1, keepdims=True)
    uc = u - mu
    var = jnp.mean(uc * uc, axis=-1, keepdims=True)
    o_ref[...] = _silu(uc * lax.rsqrt(var + LN_EPS) * g_ref[...] + b_ref[...]).astype(o_ref.dtype)


def _conf(p, prm, bsz, seq):
    rows = CONF_TILE
    tiles = seq // rows
    n = bsz * seq
    fix = lambda bi, i: (0, 0)
    vec = pl.BlockSpec((1, CONF_WIDTH), fix)
    return pl.pallas_call(
        _conf_kernel,
        grid=(bsz, tiles),
        in_specs=[pl.BlockSpec((rows, 2 * CONF_WIDTH), lambda bi, i: (bi * tiles + i, 0)),
                  pl.BlockSpec((CONF_CONV, CONF_WIDTH), fix), vec, vec, vec],
        out_specs=pl.BlockSpec((rows, CONF_WIDTH), lambda bi, i: (bi * tiles + i, 0)),
        out_shape=jax.ShapeDtypeStruct((n, CONF_WIDTH), BF16),
        scratch_shapes=[pltpu.VMEM((CONF_HALO + rows, CONF_WIDTH), F32),
                        pltpu.VMEM((CONF_HALO + rows, CONF_WIDTH), F32)],
        compiler_params=_cparams(("arbitrary", "arbitrary"), 32 << 20),
    )(p, *prm)


def _ssd_kernel(z_ref, xs_ref, bc_ref, dt_ref, cwx_ref, cbx_ref, cwb_ref, cbb_ref, dtb_ref, alog_ref,
                dskip_ref, normg_ref, expand_ref, tril_ref, o_ref, ext_xs, ext_bc, state_ref, ybuf_ref):
    L = SSM_CHUNK
    P = SSM_HEAD_DIM
    NS = SSM_STATE
    HG = SSM_HEADS // SSM_GROUPS
    first = pl.program_id(1) == 0

    @pl.when(first)
    def _():
        state_ref[...] = jnp.zeros(state_ref.shape, F32)

    xs = _silu(_causal_conv(ext_xs, xs_ref[...], cwx_ref, cbx_ref, first, L, SSM_CONV, SUBLANES))
    bc = _silu(_causal_conv(ext_bc, bc_ref[...], cwb_ref, cbb_ref, first, L, SSM_CONV, SUBLANES))
    dt = _softplus(dt_ref[...] + dtb_ref[...])
    a_head = -jnp.exp(alog_ref[...])
    acum = _dot_onehot_lhs(tril_ref[...], dt * a_head, 3)
    acum_t = acum.T
    acum_x = _dot_onehot_rhs(acum, expand_ref[...], 3)
    acum_xl = acum_x[L - 1:L, :]
    xdt = xs * _dot_onehot_rhs(dt, expand_ref[...], 3)
    causal = lax.broadcasted_iota(jnp.int32, (L, L), 1) <= lax.broadcasted_iota(jnp.int32, (L, L), 0)

    half = SSM_INNER // SSM_GROUPS
    x_end = (xdt * jnp.exp(acum_xl - acum_x)).astype(BF16)
    state = state_ref[...]
    cb = []
    for g in range(SSM_GROUPS):
        gs = slice(g * half, (g + 1) * half)
        bm = bc[:, g * NS:(g + 1) * NS]
        cm = bc[:, (SSM_GROUPS + g) * NS:(SSM_GROUPS + g + 1) * NS].astype(BF16)
        cb.append(_dot_nt(cm, bm))
        ybuf_ref[:, gs] = _dot(cm, state[:, gs])
        state_ref[:, gs] = state[:, gs] * jnp.exp(acum_xl[:, gs]) + _dot(bm.T, x_end[:, gs])
    y_off = ybuf_ref[...] * jnp.exp(acum_x)
    for h0 in range(0, SSM_HEADS, SSM_HEAD_BATCH):
        hs = range(h0, h0 + SSM_HEAD_BATCH)
        decay = {h: jnp.exp(jnp.where(causal, acum[:, h:h + 1] - acum_t[h:h + 1, :], -jnp.inf)) for h in hs}
        y_diag = {h: _dot(cb[h // HG] * decay[h], xdt[:, h * P:(h + 1) * P]) for h in hs}
        for h in hs:
            ybuf_ref[:, h * P:(h + 1) * P] = y_diag[h]

    y = (ybuf_ref[...] + y_off + dskip_ref[...] * xs) * _silu(z_ref[...])
    for g in range(SSM_GROUPS):
        yg = y[:, g * half:(g + 1) * half]
        ms = jnp.mean(yg * yg, axis=-1, keepdims=True)
        o_ref[:, g * half:(g + 1) * half] = (yg * lax.rsqrt(ms + LN_EPS)
                                             * normg_ref[:, g * half:(g + 1) * half]).astype(o_ref.dtype)


def _ssd(p, prm, bsz, seq):
    L = SSM_CHUNK
    tiles = seq // L
    n = bsz * seq
    row = lambda c: (lambda bi, i: (bi * tiles + i, c))
    fix = lambda bi, i: (0, 0)
    bcw = 2 * SSM_GROUPS * SSM_STATE
    vec = lambda width: pl.BlockSpec((1, width), fix)
    in_specs = [pl.BlockSpec((L, SSM_INNER), row(1)), pl.BlockSpec((L, SSM_INNER), row(2)),
                pl.BlockSpec((L, bcw), row(6)), pl.BlockSpec((L, LANES), row(28)),
                pl.BlockSpec((SSM_CONV, SSM_INNER), fix), vec(SSM_INNER),
                pl.BlockSpec((SSM_CONV, bcw), fix), vec(bcw),
                vec(LANES), vec(LANES), vec(SSM_INNER), vec(SSM_INNER),
                pl.BlockSpec((LANES, SSM_INNER), fix), pl.BlockSpec((L, L), fix)]
    return pl.pallas_call(
        _ssd_kernel,
        grid=(bsz, tiles),
        in_specs=in_specs,
        out_specs=pl.BlockSpec((L, SSM_INNER), lambda bi, i: (bi * tiles + i, 0)),
        out_shape=jax.ShapeDtypeStruct((n, SSM_INNER), BF16),
        scratch_shapes=[pltpu.VMEM((SUBLANES + L, SSM_INNER), F32),
                        pltpu.VMEM((SUBLANES + L, bcw), F32),
                        pltpu.VMEM((SSM_STATE, SSM_INNER), F32),
                        pltpu.VMEM((L, SSM_INNER), F32)],
        compiler_params=_cparams(("arbitrary", "arbitrary"), 32 << 20),
    )(p, p, p, p, *prm)


def _row(v):
    return v.reshape(1, -1).astype(F32)


def _pad_rows(m, rows, offset):
    out = jnp.zeros((rows, m.shape[1]), F32)
    return out.at[offset:offset + m.shape[0]].set(m)


def _block_diag(w):
    nb, d, e = w.shape
    eye = jnp.eye(nb, dtype=w.dtype)
    return (eye[:, None, :, None] * w[:, :, None, :]).reshape(nb * d, nb * e)


def _tril_ones(n, blocks=1):
    return jnp.asarray(np.kron(np.eye(blocks, dtype=np.float32), np.tril(np.ones((n, n), np.float32))), dtype=BF16)


def _head_sum_matrix():
    idx = np.arange(RWKV_GROUP * RWKV_HEAD_DIM) // RWKV_HEAD_DIM
    return jnp.asarray((idx[:, None] == idx[None, :]).astype(np.float32), dtype=BF16)


def _head_expand_matrix():
    m = np.zeros((LANES, SSM_INNER), np.float32)
    for h in range(SSM_HEADS):
        m[h, h * SSM_HEAD_DIM:(h + 1) * SSM_HEAD_DIM] = 1.0
    return jnp.asarray(m, dtype=BF16)


def _sublayers(h, mem2, i, bsz, xa_wq, xa_wk, xa_wv, xa_wo, ffn_w1, ffn_w2, ln_mem_g, ln_mem_b, ln_ffn_g, ln_ffn_b):
    kmem = _proj(mem2, xa_wk[i].astype(BF16), mem2.shape[0] // bsz, BF16)
    vmem_ = _proj(mem2, xa_wv[i].astype(BF16), mem2.shape[0] // bsz, BF16)
    h = _attn(h, kmem, vmem_, xa_wq[i].astype(BF16), xa_wo[i].astype(BF16), _row(ln_mem_g[i]), _row(ln_mem_b[i]),
              bsz, 512)
    return _ffn(h, ffn_w1[i].astype(BF16), ffn_w2[i].astype(BF16), _row(ln_ffn_g[i]), _row(ln_ffn_b[i]), 512)


def kernel(x, mem, ev_w_in, ev_mu, ev_w0, ev_w2, ev_a0, ev_a2, ev_g2, ev_k_k, ev_k_a, ev_r_k, ev_gn_g, ev_gn_b, ev_lru_conv_w, ev_lru_conv_b, ev_lru_wa, ev_lru_ba, ev_lru_wx, ev_lru_bx, ev_lru_lam, ev_w_out, od_w_in, od_cf_conv_w, od_cf_conv_b, od_cf_ln_g, od_cf_ln_b, od_ssm_conv_w, od_ssm_conv_b, od_dt_bias, od_a_log, od_d_skip, od_ssm_norm_g, od_w_out, xa_wq, xa_wk, xa_wv, xa_wo, ffn_w1, ffn_w2, ln_mix_g, ln_mix_b, ln_mem_g, ln_mem_b, ln_ffn_g, ln_ffn_b):
    bsz, seq, _ = x.shape
    n = bsz * seq
    h = x.reshape(n, D_MODEL)
    mem2 = mem.reshape(bsz * mem.shape[1], D_MODEL)
    xa = (xa_wq, xa_wk, xa_wv, xa_wo, ffn_w1, ffn_w2, ln_mem_g, ln_mem_b, ln_ffn_g, ln_ffn_b)

    rw = 3 * RWKV_WIDTH
    lo = 2 * LANES
    w_in = jnp.concatenate([ev_w_in[0][:, 0:rw], ev_w_in[0][:, rw + lo:], ev_w_in[0][:, rw:rw + lo]], axis=1)
    p = _proj(h, w_in.astype(BF16), 512)
    mu = ev_mu[0]
    rwkv_prm = (_row(mu[0:rw]), _row(mu[rw:rw + LANES]), _row(mu[rw + LANES:rw + lo]),
                _row(ev_w0[0]), _pad_rows(ev_w2[0], LANES, 0), _row(ev_a0[0]), _pad_rows(ev_a2[0], LANES, 64),
                ev_g2[0].astype(F32), _row(ev_k_k[0]), _row(ev_k_a[0]), _row(ev_r_k[0]), _row(ev_gn_g[0]),
                _row(ev_gn_b[0]), _head_sum_matrix(), _tril_ones(RWKV_CHUNK, RWKV_STEP_CHUNKS))
    lru_prm = (ev_lru_conv_w[0], _row(ev_lru_conv_b[0]), _block_diag(ev_lru_wa[0]).astype(BF16), _row(ev_lru_ba[0]),
               _block_diag(ev_lru_wx[0]).astype(BF16), _row(ev_lru_bx[0]), _row(ev_lru_lam[0]))
    y_a, y_b = _mix0(p, rwkv_prm, lru_prm, bsz, seq)
    w_out = ev_w_out[0].astype(BF16)
    h = _mix_out(y_a, y_b, w_out[0:RWKV_WIDTH], w_out[RWKV_WIDTH:], h, _row(ln_mix_g[0]), _row(ln_mix_b[0]), 512)
    h = _sublayers(h, mem2, 0, bsz, *xa)

    w_in = jnp.pad(od_w_in[0], ((0, 0), (0, ODD_IN_PAD - ODD_IN))).astype(BF16)
    p = _proj(h, w_in, 256)
    conf_prm = (od_cf_conv_w[0], _row(od_cf_conv_b[0]), _row(od_cf_ln_g[0]), _row(od_cf_ln_b[0]))
    y_c = _conf(p, conf_prm, bsz, seq)
    cw, cb = od_ssm_conv_w[0], od_ssm_conv_b[0]
    pad16 = lambda v: jnp.pad(v, (0, LANES - SSM_HEADS)).reshape(1, LANES).astype(F32)
    ssd_prm = (cw[:, 0:SSM_INNER], _row(cb[0:SSM_INNER]), cw[:, SSM_INNER:], _row(cb[SSM_INNER:]),
               pad16(od_dt_bias[0]), pad16(od_a_log[0]), _row(jnp.repeat(od_d_skip[0], SSM_HEAD_DIM)),
               _row(od_ssm_norm_g[0]), _head_expand_matrix(), _tril_ones(SSM_CHUNK))
    y_d = _ssd(p, ssd_prm, bsz, seq)
    w_out = od_w_out[0].astype(BF16)
    h = _mix_out(y_c, y_d, w_out[0:CONF_WIDTH], w_out[CONF_WIDTH:], h, _row(ln_mix_g[1]), _row(ln_mix_b[1]), 512)
    h = _sublayers(h, mem2, 1, bsz, *xa)
    return h.reshape(bsz, seq, D_MODEL)
```

```python
import jax
import jax.numpy as jnp
import numpy as np
from jax import lax
from jax.experimental import pallas as pl
from jax.experimental.pallas import tpu as pltpu

F32 = jnp.float32
BF16 = jnp.bfloat16

D_MODEL = 1024
DEPTH = 2
DN_ALPHA = (2 * DEPTH) ** 0.25
LN_EPS = 1e-5

RWKV_WIDTH = 512
RWKV_HEAD_DIM = 64
RWKV_HEADS = 8
RWKV_GN_EPS = 64e-5
LOG2_E = 1.4426950408889634
HALF_DECAY = 0.6065306597126334
RWKV_CHUNK = 64
RWKV_STEP_CHUNKS = 4
RWKV_GROUP = 4

LRU_WIDTH = 512
LRU_C = 8.0
LRU_CONV = 4
LRU_TILE = 256

CONF_WIDTH = 512
CONF_CONV = 31
CONF_HALO = 32
CONF_TILE = 256

SSM_INNER = 1024
SSM_HEAD_DIM = 64
SSM_HEADS = 16
SSM_GROUPS = 2
SSM_STATE = 128
SSM_CONV = 4
SSM_CHUNK = 128
SSM_HEAD_BATCH = 4
ODD_IN = 3600
ODD_IN_PAD = 3712

EPILOGUE_SPLIT = 2
XA_HEADS = 4
XA_HEAD_DIM = 256
D_FF = 4096

SUBLANES = 8
LANES = 128
VMEM_CAP = 56 * 1024 * 1024


def _cparams(semantics, vmem_bytes):
    return pltpu.CompilerParams(dimension_semantics=semantics,
                                vmem_limit_bytes=int(min(max(vmem_bytes, 16 * 1024 * 1024), VMEM_CAP)))


def _dot(a, b):
    return jnp.dot(a.astype(BF16), b.astype(BF16), preferred_element_type=F32)


def _split_parts(x, parts):
    out = []
    for _ in range(parts - 1):
        hi = x.astype(BF16)
        out.append(hi)
        x = x - hi.astype(F32)
    out.append(x.astype(BF16))
    return out


def _dot_onehot_rhs(x, m, parts):
    acc = None
    for xp in _split_parts(x, parts):
        t = jnp.dot(xp, m, preferred_element_type=F32)
        acc = t if acc is None else acc + t
    return acc


def _dot_onehot_lhs(m, x, parts):
    acc = None
    for xp in _split_parts(x, parts):
        t = jnp.dot(m, xp, preferred_element_type=F32)
        acc = t if acc is None else acc + t
    return acc


def _head_sums(terms, bdm):
    pieces, owner = [], []
    for n, (x, parts) in enumerate(terms):
        for p in _split_parts(x, parts):
            pieces.append(p)
            owner.append(n)
    rows = terms[0][0].shape[0]
    w = bdm.shape[0]
    stacked = jnp.concatenate(pieces, axis=0)
    sums = jnp.concatenate([jnp.dot(stacked[:, q * w:(q + 1) * w], bdm, preferred_element_type=F32)
                            for q in range(stacked.shape[1] // w)], axis=1)
    out = [None] * len(terms)
    for j, n in enumerate(owner):
        piece = sums[j * rows:(j + 1) * rows]
        out[n] = piece if out[n] is None else out[n] + piece
    return out


def _dot_nt(a, b):
    return lax.dot_general(a.astype(BF16), b.astype(BF16), (((1,), (1,)), ((), ())),
                           preferred_element_type=F32)


def _sigmoid(x):
    return 1.0 / (1.0 + jnp.exp(-x))


def _softplus(x):
    return jnp.maximum(x, 0.0) + jnp.log(1.0 + jnp.exp(-jnp.abs(x)))


def _silu(x):
    return x * _sigmoid(x)


def _gelu_tanh(x):
    return 0.5 * x * (1.0 + jnp.tanh(0.7978845608028654 * (x + 0.044715 * (x * x * x))))


def _res_ln(h, y, g, b):
    z = DN_ALPHA * h + y
    mu = jnp.mean(z, axis=-1, keepdims=True)
    zc = z - mu
    var = jnp.mean(zc * zc, axis=-1, keepdims=True)
    return zc * lax.rsqrt(var + LN_EPS) * g + b


def _proj_kernel(x_ref, w_ref, o_ref):
    o_ref[...] = _dot(x_ref[...], w_ref[...]).astype(o_ref.dtype)


def _proj(x, w, tm, out_dtype=F32):
    n, k = x.shape
    m = w.shape[1]
    vmem = 2 * (tm * k * x.dtype.itemsize + k * m * 2 + tm * m * jnp.dtype(out_dtype).itemsize) + (4 << 20)
    return pl.pallas_call(
        _proj_kernel,
        grid=(n // tm,),
        in_specs=[pl.BlockSpec((tm, k), lambda i: (i, 0)), pl.BlockSpec((k, m), lambda i: (0, 0))],
        out_specs=pl.BlockSpec((tm, m), lambda i: (i, 0)),
        out_shape=jax.ShapeDtypeStruct((n, m), out_dtype),
        compiler_params=_cparams(("parallel",), vmem),
    )(x, w)


def _mix_out_kernel(a_ref, b_ref, wa_ref, wb_ref, h_ref, g_ref, bias_ref, o_ref):
    rows = o_ref.shape[0] // EPILOGUE_SPLIT
    ys = []
    for s in range(EPILOGUE_SPLIT):
        sl = slice(s * rows, (s + 1) * rows)
        ys.append(_dot(a_ref[sl, :], wa_ref[...]) + _dot(b_ref[sl, :], wb_ref[...]))
    for s in range(EPILOGUE_SPLIT):
        sl = slice(s * rows, (s + 1) * rows)
        o_ref[sl, :] = _res_ln(h_ref[sl, :], ys[s], g_ref[...], bias_ref[...])


def _mix_out(a, b, wa, wb, h, g, bias, tm):
    n = h.shape[0]
    ka, kb = a.shape[1], b.shape[1]
    vmem = 2 * (tm * (ka + kb) * 2 + 2 * tm * D_MODEL * 4 + (ka + kb) * D_MODEL * 2) + (8 << 20)
    row = lambda i: (i, 0)
    fix = lambda i: (0, 0)
    return pl.pallas_call(
        _mix_out_kernel,
        grid=(n // tm,),
        in_specs=[pl.BlockSpec((tm, ka), row), pl.BlockSpec((tm, kb), row),
                  pl.BlockSpec((ka, D_MODEL), fix), pl.BlockSpec((kb, D_MODEL), fix),
                  pl.BlockSpec((tm, D_MODEL), row), pl.BlockSpec((1, D_MODEL), fix),
                  pl.BlockSpec((1, D_MODEL), fix)],
        out_specs=pl.BlockSpec((tm, D_MODEL), row),
        out_shape=jax.ShapeDtypeStruct((n, D_MODEL), F32),
        compiler_params=_cparams(("parallel",), vmem),
    )(a, b, wa, wb, h, g, bias)


def _attn_kernel(h_ref, k_ref, v_ref, wq_ref, wo_ref, g_ref, b_ref, o_ref, cat_ref):
    rows = o_ref.shape[0] // EPILOGUE_SPLIT
    subs = [slice(s * rows, (s + 1) * rows) for s in range(EPILOGUE_SPLIT)]
    q = [_dot(h_ref[rs, :], wq_ref[...]) for rs in subs]
    for hd in range(XA_HEADS):
        sl = slice(hd * XA_HEAD_DIM, (hd + 1) * XA_HEAD_DIM)
        for n, rs in enumerate(subs):
            s = _dot_nt(q[n][:, sl], k_ref[:, sl]) * (XA_HEAD_DIM ** -0.5)
            s = s - jnp.max(s, axis=-1, keepdims=True)
            e = jnp.exp(s)
            p = e / jnp.sum(e, axis=-1, keepdims=True)
            cat_ref[rs, sl] = _dot(p, v_ref[:, sl])
    y = [_dot(cat_ref[rs, :], wo_ref[...]) for rs in subs]
    for n, rs in enumerate(subs):
        o_ref[rs, :] = _res_ln(h_ref[rs, :], y[n], g_ref[...], b_ref[...])


def _attn(h, kmem, vmem_, wq, wo, g, b, bsz, tm):
    n = h.shape[0]
    tiles = n // bsz // tm
    n_mem = kmem.shape[0] // bsz
    vmem = 2 * (2 * tm * D_MODEL * 4 + 2 * n_mem * D_MODEL * 2 + 2 * D_MODEL * D_MODEL * 2) + tm * D_MODEL * 16 + (8 << 20)
    row = lambda bi, i: (bi * tiles + i, 0)
    mem = lambda bi, i: (bi, 0)
    fix = lambda bi, i: (0, 0)
    return pl.pallas_call(
        _attn_kernel,
        grid=(bsz, tiles),
        in_specs=[pl.BlockSpec((tm, D_MODEL), row), pl.BlockSpec((n_mem, D_MODEL), mem),
                  pl.BlockSpec((n_mem, D_MODEL), mem), pl.BlockSpec((D_MODEL, D_MODEL), fix),
                  pl.BlockSpec((D_MODEL, D_MODEL), fix), pl.BlockSpec((1, D_MODEL), fix),
                  pl.BlockSpec((1, D_MODEL), fix)],
        out_specs=pl.BlockSpec((tm, D_MODEL), row),
        out_shape=jax.ShapeDtypeStruct((n, D_MODEL), F32),
        scratch_shapes=[pltpu.VMEM((tm, D_MODEL), F32)],
        compiler_params=_cparams(("parallel", "parallel"), vmem),
    )(h, kmem, vmem_, wq, wo, g, b)


def _ffn_kernel(h_ref, w1_ref, w2_ref, g_ref, b_ref, o_ref):
    rows = o_ref.shape[0] // EPILOGUE_SPLIT
    subs = [slice(s * rows, (s + 1) * rows) for s in range(EPILOGUE_SPLIT)]
    u = [jnp.square(jnp.maximum(_dot(h_ref[rs, :], w1_ref[...]), 0.0)).astype(BF16) for rs in subs]
    y = [_dot(u[n], w2_ref[...]) for n in range(EPILOGUE_SPLIT)]
    for n, rs in enumerate(subs):
        o_ref[rs, :] = _res_ln(h_ref[rs, :], y[n], g_ref[...], b_ref[...])


def _ffn(h, w1, w2, g, b, tm):
    n = h.shape[0]
    vmem = 2 * D_MODEL * D_FF * 2 + 4 * tm * D_MODEL * 4 + tm * D_FF * 6 + tm * D_MODEL * 6 + (6 << 20)
    fix = lambda i: (0, 0)
    return pl.pallas_call(
        _ffn_kernel,
        grid=(n // tm,),
        in_specs=[pl.BlockSpec((tm, D_MODEL), lambda i: (i, 0)),
                  pl.BlockSpec((D_MODEL, D_FF), fix, pipeline_mode=pl.Buffered(1)),
                  pl.BlockSpec((D_FF, D_MODEL), fix, pipeline_mode=pl.Buffered(1)),
                  pl.BlockSpec((1, D_MODEL), fix),
                  pl.BlockSpec((1, D_MODEL), fix)],
        out_specs=pl.BlockSpec((tm, D_MODEL), lambda i: (i, 0)),
        out_shape=jax.ShapeDtypeStruct((n, D_MODEL), F32),
        compiler_params=_cparams(("parallel",), vmem),
    )(h, w1, w2, g, b)


def _shifted(ext_ref, x, first, rows):
    if first is not None:
        @pl.when(first)
        def _():
            ext_ref[...] = jnp.zeros(ext_ref.shape, F32)

    ext = jnp.concatenate([ext_ref[...], x], axis=0)
    prev = pltpu.roll(ext, 1, axis=0)[SUBLANES:SUBLANES + rows, :]
    ext_ref[...] = x[rows - SUBLANES:rows, :]
    return prev


def _causal_conv(ext_ref, x, w_ref, b_ref, first, rows, taps, halo):
    if first is not None:
        @pl.when(first)
        def _():
            ext_ref[...] = jnp.zeros(ext_ref.shape, F32)

    ext = jnp.concatenate([ext_ref[...], x], axis=0)
    acc = x * w_ref[taps - 1:taps, :] + b_ref[...]
    for b in range(min(SUBLANES, taps)):
        shifted = ext if b == 0 else pltpu.roll(ext, b, axis=0)
        for a in range((taps - 1 - b) // SUBLANES + 1):
            lag = SUBLANES * a + b
            if lag > 0:
                start = halo - SUBLANES * a
                acc = acc + shifted[start:start + rows, :] * w_ref[taps - 1 - lag:taps - lag, :]
    ext_ref[...] = x[rows - halo:rows, :]
    return acc


def _lru_stages(xb_ref, gb_ref, cw_ref, cb_ref, wa_ref, ba_ref, wx_ref, bx_ref, lam_ref, o_ref, ext_ref, carry_ref):
    rows = LRU_TILE
    xc = _causal_conv(ext_ref, xb_ref[...], cw_ref, cb_ref, None, rows, LRU_CONV, SUBLANES)
    yield
    gate_r = _sigmoid(_dot(xc, wa_ref[...]) + ba_ref[...])
    gate_i = _sigmoid(_dot(xc, wx_ref[...]) + bx_ref[...])
    yield
    log_a = -LRU_C * _softplus(-lam_ref[...]) * gate_r
    a = jnp.exp(log_a)
    u = jnp.sqrt(1.0 - a * a) * (gate_i * xc)
    yield
    ridx = lax.broadcasted_iota(jnp.int32, (rows, LRU_WIDTH), 0)
    s = 1
    while s < SUBLANES:
        keep = ridx >= s
        a_sh = jnp.where(keep, pltpu.roll(a, s, axis=0), 1.0)
        u_sh = jnp.where(keep, pltpu.roll(u, s, axis=0), 0.0)
        u = a * u_sh + u
        a = a * a_sh
        s *= 2
        yield
    while s < rows:
        u = jnp.concatenate([u[0:s], a[s:rows] * u[0:rows - s] + u[s:rows]], axis=0)
        a = jnp.concatenate([a[0:s], a[s:rows] * a[0:rows - s]], axis=0)
        s *= 2
        yield
    h = a * carry_ref[0:1, :] + u
    carry_ref[...] = jnp.broadcast_to(h[rows - 1:rows, :], carry_ref.shape)
    o_ref[...] = (h * _gelu_tanh(gb_ref[...])).astype(o_ref.dtype)


def _mix0_kernel(rkv_ref, wa_ref, gd_ref, mu_rkv_ref, mu_wa_ref, mu_gd_ref, w0_ref, w2_ref, a0_ref, a2_ref,
                 g2_ref, kk_ref, ka_ref, rk_ref, gng_ref, gnb_ref, bdm_ref, tril_ref,
                 xb_ref, gb_ref, cw_ref, cb_ref, lwa_ref, lba_ref, lwx_ref, lbx_ref, lam_ref,
                 o_ref, o_lru_ref,
                 ext_rkv, ext_wa, ext_gd, state_ref, obuf_ref, ext_lru, carry_lru):
    L = RWKV_CHUNK
    CH = RWKV_STEP_CHUNKS
    rows = L * CH
    N = RWKV_HEAD_DIM

    @pl.when(pl.program_id(1) == 0)
    def _():
        state_ref[...] = jnp.zeros(state_ref.shape, F32)
        carry_lru[...] = jnp.zeros(carry_lru.shape, F32)
        for ext in (ext_rkv, ext_wa, ext_gd, ext_lru):
            ext[...] = jnp.zeros(ext.shape, F32)

    lru = _lru_stages(xb_ref, gb_ref, cw_ref, cb_ref, lwa_ref, lba_ref, lwx_ref, lbx_ref, lam_ref, o_lru_ref,
                      ext_lru, carry_lru)
    tick = lambda: next(lru, None)

    x = rkv_ref[...]
    xm = x + (_shifted(ext_rkv, x, None, rows) - x) * mu_rkv_ref[...]
    wa = wa_ref[...]
    wam = wa + (_shifted(ext_wa, wa, None, rows) - wa) * mu_wa_ref[...]
    gd = gd_ref[...]
    gdm = gd + (_shifted(ext_gd, gd, None, rows) - gd) * mu_gd_ref[...]

    r = xm[:, 0:RWKV_WIDTH]
    k = xm[:, RWKV_WIDTH:2 * RWKV_WIDTH]
    v = xm[:, 2 * RWKV_WIDTH:3 * RWKV_WIDTH]
    z = w0_ref[...] + _dot(jnp.tanh(wam), w2_ref[...])
    logdec = (-LOG2_E * HALF_DECAY) * _sigmoid(z)
    a = _sigmoid(a0_ref[...] + _dot(wam, a2_ref[...]))
    g = _dot(_sigmoid(gdm), g2_ref[...])
    bdm = bdm_ref[...]
    kks = k * kk_ref[...]
    k2 = k * (1.0 + (a - 1.0) * ka_ref[...])
    kk_sq, bonus_s = _head_sums([(kks * kks, 2), (r * k2 * rk_ref[...], 1)], bdm)
    kk = kks * lax.rsqrt(jnp.maximum(kk_sq, 1e-24))
    bonus = bonus_s * v
    beta = kk * a

    cum = _dot_onehot_lhs(tril_ref[...], logdec, 2)
    cum_l = jnp.concatenate(
        [jnp.broadcast_to(cum[c * L + L - 1:c * L + L, :], (L, RWKV_WIDTH)) for c in range(CH)], axis=0)
    e_to_end = jnp.exp2(cum_l - cum)
    e_neg = jnp.exp2(-cum)
    a_t = -kk * jnp.exp2(cum - logdec)
    r_t = r * jnp.exp2(cum)
    b_t = beta * e_neg
    k_t = k2 * e_neg
    b_p = beta * e_to_end
    k_p = k2 * e_to_end
    p_l = jnp.exp2(cum_l)

    W = RWKV_GROUP * N
    bdm32 = bdm.astype(F32)
    lane_s = jnp.bitwise_and(lax.broadcasted_iota(jnp.int32, (L, W), 1), N - 1)
    t_idx = lax.broadcasted_iota(jnp.int32, (L, W), 0)
    strict = lane_s < t_idx
    incl = lane_s <= t_idx
    eye_cat = jnp.where(lane_s == t_idx, 1.0, 0.0)
    mask2 = jnp.concatenate([strict, incl], axis=0)

    per_half = LANES // N
    half_mask = [bdm[j * N:j * N + L, 0:LANES] for j in range(per_half)]
    zero_half = jnp.zeros((L, LANES), BF16)

    def bd(y):
        yb = y.astype(BF16)
        blocks = []
        for h in range(RWKV_GROUP):
            half, j = divmod(h, per_half)
            part = yb[:, half * LANES:(half + 1) * LANES] * half_mask[j]
            blocks.append(jnp.concatenate([part if c == half else zero_half for c in range(W // LANES)], axis=1))
        return jnp.concatenate(blocks, axis=0)

    mmb = lambda x, y: jnp.dot(x.astype(BF16), y, preferred_element_type=F32)
    mmb_nt = lambda x, y: lax.dot_general(x.astype(BF16), y, (((1,), (1,)), ((), ())), preferred_element_type=F32)

    items = [(c, q) for c in range(CH) for q in range(RWKV_WIDTH // W)]
    cut = lambda arr, c, q: arr[c * L:(c + 1) * L, q * W:(q + 1) * W]
    a_c = {i: cut(a_t, *i) for i in items}
    r_c = {i: cut(r_t, *i) for i in items}
    v_c = {i: cut(v, *i) for i in items}
    bp_c = {i: cut(b_p, *i) for i in items}
    lhs = {i: jnp.concatenate([a_c[i], r_c[i]], axis=0).astype(BF16) for i in items}
    q_b = {i: mmb_nt(lhs[i], bd(cut(b_t, *i))) for i in items}
    tick()
    q_k = {i: jnp.where(mask2, mmb_nt(lhs[i], bd(cut(k_t, *i))), 0.0) for i in items}
    tick()
    qkv = {i: mmb(q_k[i], bd(v_c[i])) for i in items}
    tick()
    pw = {i: jnp.where(strict, q_b[i][0:L], 0.0) for i in items}
    tinv = {i: eye_cat + pw[i] for i in items}
    pw = {i: mmb(pw[i], bd(pw[i])) for i in items}
    tick()
    for _ in range(L.bit_length() - 3):
        both = {i: mmb(jnp.concatenate([pw[i], tinv[i]], axis=0), bd(pw[i])) for i in items}
        pw = {i: both[i][0:L] for i in items}
        tinv = {i: tinv[i] + both[i][L:2 * L] for i in items}
        tick()
    tinv = {i: tinv[i] + mmb(tinv[i], bd(pw[i])) for i in items}
    tick()
    ta = {i: mmb(tinv[i], bd(a_c[i])) for i in items}
    w2 = {i: mmb(tinv[i], bd(qkv[i][0:L])) for i in items}
    tick()
    a_rb = {i: jnp.where(incl, q_b[i][L:2 * L], 0.0).astype(BF16) for i in items}
    q_t = {i: r_c[i] + mmb(a_rb[i], bd(ta[i])) for i in items}
    o_loc = {i: mmb(a_rb[i], bd(w2[i])) + qkv[i][L:2 * L] for i in items}
    tick()
    gk = {i: _dot(ta[i].T, bp_c[i]).astype(BF16) * bdm for i in items}
    s_loc = {i: _dot(jnp.concatenate([w2[i], v_c[i]], axis=0).T,
                     jnp.concatenate([bp_c[i], cut(k_p, *i)], axis=0)) * bdm32 for i in items}
    tick()
    state = [state_ref[q] for q in range(RWKV_WIDTH // W)]
    for c in range(CH):
        for q in range(RWKV_WIDTH // W):
            i = (c, q)
            obuf_ref[c * L:(c + 1) * L, q * W:(q + 1) * W] = mmb_nt(q_t[i], state[q].astype(BF16)) + o_loc[i]
            state[q] = state[q] * p_l[c * L:c * L + 1, q * W:(q + 1) * W] + mmb(state[q], gk[i]) + s_loc[i]
        tick()
    for q in range(RWKV_WIDTH // W):
        state_ref[q] = state[q]

    o = obuf_ref[...]
    oc = o - _head_sums([(o, 1)], bdm)[0] * (1.0 / N)
    var = _head_sums([(oc * oc, 1)], bdm)[0] * (1.0 / N)
    o_ref[...] = ((oc * lax.rsqrt(var + RWKV_GN_EPS) * gng_ref[...] + gnb_ref[...] + bonus) * g).astype(o_ref.dtype)
    for _ in lru:
        pass


def _mix0(p, rwkv_prm, lru_prm, bsz, seq):
    L = RWKV_CHUNK * RWKV_STEP_CHUNKS
    assert L == LRU_TILE
    tiles = seq // L
    n = bsz * seq
    gw = RWKV_GROUP * RWKV_HEAD_DIM
    row = lambda c: (lambda bi, i: (bi * tiles + i, c))
    fix = lambda bi, i: (0, 0)
    vec = lambda width: pl.BlockSpec((1, width), fix)
    mat = lambda r, c: pl.BlockSpec((r, c), fix)
    in_specs = [pl.BlockSpec((L, 3 * RWKV_WIDTH), row(0)),
                pl.BlockSpec((L, LANES), row(20)), pl.BlockSpec((L, LANES), row(21)),
                vec(3 * RWKV_WIDTH), vec(LANES), vec(LANES),
                vec(RWKV_WIDTH), mat(LANES, RWKV_WIDTH), vec(RWKV_WIDTH), mat(LANES, RWKV_WIDTH),
                mat(LANES, RWKV_WIDTH),
                vec(RWKV_WIDTH), vec(RWKV_WIDTH), vec(RWKV_WIDTH), vec(RWKV_WIDTH), vec(RWKV_WIDTH),
                mat(gw, gw), mat(L, L),
                pl.BlockSpec((L, LRU_WIDTH), row(3)), pl.BlockSpec((L, LRU_WIDTH), row(4)),
                mat(LRU_CONV, LRU_WIDTH), vec(LRU_WIDTH), mat(LRU_WIDTH, LRU_WIDTH), vec(LRU_WIDTH),
                mat(LRU_WIDTH, LRU_WIDTH), vec(LRU_WIDTH), vec(LRU_WIDTH)]
    out_row = lambda bi, i: (bi * tiles + i, 0)
    return pl.pallas_call(
        _mix0_kernel,
        grid=(bsz, tiles),
        in_specs=in_specs,
        out_specs=[pl.BlockSpec((L, RWKV_WIDTH), out_row), pl.BlockSpec((L, LRU_WIDTH), out_row)],
        out_shape=[jax.ShapeDtypeStruct((n, RWKV_WIDTH), BF16), jax.ShapeDtypeStruct((n, LRU_WIDTH), BF16)],
        scratch_shapes=[pltpu.VMEM((SUBLANES, 3 * RWKV_WIDTH), F32),
                        pltpu.VMEM((SUBLANES, LANES), F32),
                        pltpu.VMEM((SUBLANES, LANES), F32),
                        pltpu.VMEM((RWKV_HEADS // RWKV_GROUP, gw, gw), F32),
                        pltpu.VMEM((L, RWKV_WIDTH), F32),
                        pltpu.VMEM((SUBLANES, LRU_WIDTH), F32),
                        pltpu.VMEM((SUBLANES, LRU_WIDTH), F32)],
        compiler_params=_cparams(("arbitrary", "arbitrary"), 48 << 20),
    )(p, p, p, *rwkv_prm, p, p, *lru_prm)


def _conf_kernel(c_ref, cw_ref, cb_ref, g_ref, b_ref, o_ref, ext_ref):
    rows = CONF_TILE
    first = pl.program_id(1) == 0
    c = c_ref[...]
    glu = c[:, 0:CONF_WIDTH] * _sigmoid(c[:, CONF_WIDTH:2 * CONF_WIDTH])
    u = _causal_conv(ext_ref, glu, cw_ref, cb_ref, first, rows, CONF_CONV, CONF_HALO)
    mu = jnp.mean(u, axis=-1, keepdims=True)
    uc = u - mu
    var = jnp.mean(uc * uc, axis=-1, keepdims=True)
    o_ref[...] = _silu(uc * lax.rsqrt(var + LN_EPS) * g_ref[...] + b_ref[...]).astype(o_ref.dtype)


def _conf(p, prm, bsz, seq):
    rows = CONF_TILE
    tiles = seq // rows
    n = bsz * seq
    fix = lambda bi, i: (0, 0)
    vec = pl.BlockSpec((1, CONF_WIDTH), fix)
    return pl.pallas_call(
        _conf_kernel,
        grid=(bsz, tiles),
        in_specs=[pl.BlockSpec((rows, 2 * CONF_WIDTH), lambda bi, i: (bi * tiles + i, 0)),
                  pl.BlockSpec((CONF_CONV, CONF_WIDTH), fix), vec, vec, vec],
        out_specs=pl.BlockSpec((rows, CONF_WIDTH), lambda bi, i: (bi * tiles + i, 0)),
        out_shape=jax.ShapeDtypeStruct((n, CONF_WIDTH), BF16),
        scratch_shapes=[pltpu.VMEM((CONF_HALO, CONF_WIDTH), F32)],
        compiler_params=_cparams(("arbitrary", "arbitrary"), 32 << 20),
    )(p, *prm)


def _ssd_kernel(z_ref, xs_ref, bc_ref, dt_ref, cwx_ref, cbx_ref, cwb_ref, cbb_ref, dtb_ref, alog_ref,
                dskip_ref, normg_ref, expand_ref, tril_ref, o_ref, ext_xs, ext_bc, state_ref, ybuf_ref):
    L = SSM_CHUNK
    P = SSM_HEAD_DIM
    NS = SSM_STATE
    HG = SSM_HEADS // SSM_GROUPS
    first = pl.program_id(1) == 0

    @pl.when(first)
    def _():
        state_ref[...] = jnp.zeros(state_ref.shape, F32)

    xs = _silu(_causal_conv(ext_xs, xs_ref[...], cwx_ref, cbx_ref, first, L, SSM_CONV, SUBLANES))
    bc = _silu(_causal_conv(ext_bc, bc_ref[...], cwb_ref, cbb_ref, first, L, SSM_CONV, SUBLANES))
    dt = _softplus(dt_ref[...] + dtb_ref[...])
    a_head = -jnp.exp(alog_ref[...])
    acum = _dot_onehot_lhs(tril_ref[...], dt * (a_head * LOG2_E), 3)
    acum_t = acum.T
    acum_x = _dot_onehot_rhs(acum, expand_ref[...], 3)
    acum_xl = acum_x[L - 1:L, :]
    xdt = xs * _dot_onehot_rhs(dt, expand_ref[...], 3)
    causal = lax.broadcasted_iota(jnp.int32, (L, L), 1) <= lax.broadcasted_iota(jnp.int32, (L, L), 0)

    half = SSM_INNER // SSM_GROUPS
    x_end = (xdt * jnp.exp2(acum_xl - acum_x)).astype(BF16)
    state = state_ref[...]
    cb = []
    for g in range(SSM_GROUPS):
        gs = slice(g * half, (g + 1) * half)
        bm = bc[:, g * NS:(g + 1) * NS]
        cm = bc[:, (SSM_GROUPS + g) * NS:(SSM_GROUPS + g + 1) * NS].astype(BF16)
        cb.append(_dot_nt(cm, bm))
        ybuf_ref[:, gs] = _dot(cm, state[:, gs])
        state_ref[:, gs] = state[:, gs] * jnp.exp2(acum_xl[:, gs]) + _dot(bm.T, x_end[:, gs])
    y_off = ybuf_ref[...] * jnp.exp2(acum_x)
    for h0 in range(0, SSM_HEADS, SSM_HEAD_BATCH):
        hs = range(h0, h0 + SSM_HEAD_BATCH)
        decay = {h: jnp.exp2(jnp.where(causal, acum[:, h:h + 1] - acum_t[h:h + 1, :], -jnp.inf)) for h in hs}
        y_diag = {h: _dot(cb[h // HG] * decay[h], xdt[:, h * P:(h + 1) * P]) for h in hs}
        for h in hs:
            ybuf_ref[:, h * P:(h + 1) * P] = y_diag[h]

    y = (ybuf_ref[...] + y_off + dskip_ref[...] * xs) * _silu(z_ref[...])
    for g in range(SSM_GROUPS):
        yg = y[:, g * half:(g + 1) * half]
        ms = jnp.mean(yg * yg, axis=-1, keepdims=True)
        o_ref[:, g * half:(g + 1) * half] = (yg * lax.rsqrt(ms + LN_EPS)
                                             * normg_ref[:, g * half:(g + 1) * half]).astype(o_ref.dtype)


def _ssd(p, prm, bsz, seq):
    L = SSM_CHUNK
    tiles = seq // L
    n = bsz * seq
    row = lambda c: (lambda bi, i: (bi * tiles + i, c))
    fix = lambda bi, i: (0, 0)
    bcw = 2 * SSM_GROUPS * SSM_STATE
    vec = lambda width: pl.BlockSpec((1, width), fix)
    in_specs = [pl.BlockSpec((L, SSM_INNER), row(1)), pl.BlockSpec((L, SSM_INNER), row(2)),
                pl.BlockSpec((L, bcw), row(6)), pl.BlockSpec((L, LANES), row(28)),
                pl.BlockSpec((SSM_CONV, SSM_INNER), fix), vec(SSM_INNER),
                pl.BlockSpec((SSM_CONV, bcw), fix), vec(bcw),
                vec(LANES), vec(LANES), vec(SSM_INNER), vec(SSM_INNER),
                pl.BlockSpec((LANES, SSM_INNER), fix), pl.BlockSpec((L, L), fix)]
    return pl.pallas_call(
        _ssd_kernel,
        grid=(bsz, tiles),
        in_specs=in_specs,
        out_specs=pl.BlockSpec((L, SSM_INNER), lambda bi, i: (bi * tiles + i, 0)),
        out_shape=jax.ShapeDtypeStruct((n, SSM_INNER), BF16),
        scratch_shapes=[pltpu.VMEM((SUBLANES, SSM_INNER), F32),
                        pltpu.VMEM((SUBLANES, bcw), F32),
                        pltpu.VMEM((SSM_STATE, SSM_INNER), F32),
                        pltpu.VMEM((L, SSM_INNER), F32)],
        compiler_params=_cparams(("arbitrary", "arbitrary"), 32 << 20),
    )(p, p, p, p, *prm)


def _row(v):
    return v.reshape(1, -1).astype(F32)


def _pad_rows(m, rows, offset):
    out = jnp.zeros((rows, m.shape[1]), F32)
    return out.at[offset:offset + m.shape[0]].set(m)


def _block_diag(w):
    nb, d, e = w.shape
    eye = jnp.eye(nb, dtype=w.dtype)
    return (eye[:, None, :, None] * w[:, :, None, :]).reshape(nb * d, nb * e)


def _tril_ones(n, blocks=1):
    return jnp.asarray(np.kron(np.eye(blocks, dtype=np.float32), np.tril(np.ones((n, n), np.float32))), dtype=BF16)


def _head_sum_matrix():
    idx = np.arange(RWKV_GROUP * RWKV_HEAD_DIM) // RWKV_HEAD_DIM
    return jnp.asarray((idx[:, None] == idx[None, :]).astype(np.float32), dtype=BF16)


def _head_expand_matrix():
    m = np.zeros((LANES, SSM_INNER), np.float32)
    for h in range(SSM_HEADS):
        m[h, h * SSM_HEAD_DIM:(h + 1) * SSM_HEAD_DIM] = 1.0
    return jnp.asarray(m, dtype=BF16)


def _sublayers(h, mem2, i, bsz, xa_wq, xa_wk, xa_wv, xa_wo, ffn_w1, ffn_w2, ln_mem_g, ln_mem_b, ln_ffn_g, ln_ffn_b):
    kmem = _proj(mem2, xa_wk[i].astype(BF16), mem2.shape[0] // bsz, BF16)
    vmem_ = _proj(mem2, xa_wv[i].astype(BF16), mem2.shape[0] // bsz, BF16)
    h = _attn(h, kmem, vmem_, xa_wq[i].astype(BF16), xa_wo[i].astype(BF16), _row(ln_mem_g[i]), _row(ln_mem_b[i]),
              bsz, 512)
    return _ffn(h, ffn_w1[i].astype(BF16), ffn_w2[i].astype(BF16), _row(ln_ffn_g[i]), _row(ln_ffn_b[i]), 512)


def kernel(x, mem, ev_w_in, ev_mu, ev_w0, ev_w2, ev_a0, ev_a2, ev_g2, ev_k_k, ev_k_a, ev_r_k, ev_gn_g, ev_gn_b, ev_lru_conv_w, ev_lru_conv_b, ev_lru_wa, ev_lru_ba, ev_lru_wx, ev_lru_bx, ev_lru_lam, ev_w_out, od_w_in, od_cf_conv_w, od_cf_conv_b, od_cf_ln_g, od_cf_ln_b, od_ssm_conv_w, od_ssm_conv_b, od_dt_bias, od_a_log, od_d_skip, od_ssm_norm_g, od_w_out, xa_wq, xa_wk, xa_wv, xa_wo, ffn_w1, ffn_w2, ln_mix_g, ln_mix_b, ln_mem_g, ln_mem_b, ln_ffn_g, ln_ffn_b):
    bsz, seq, _ = x.shape
    n = bsz * seq
    h = x.reshape(n, D_MODEL)
    mem2 = mem.reshape(bsz * mem.shape[1], D_MODEL)
    xa = (xa_wq, xa_wk, xa_wv, xa_wo, ffn_w1, ffn_w2, ln_mem_g, ln_mem_b, ln_ffn_g, ln_ffn_b)

    rw = 3 * RWKV_WIDTH
    lo = 2 * LANES
    w_in = jnp.concatenate([ev_w_in[0][:, 0:rw], ev_w_in[0][:, rw + lo:], ev_w_in[0][:, rw:rw + lo]], axis=1)
    p = _proj(h, w_in.astype(BF16), 512)
    mu = ev_mu[0]
    rwkv_prm = (_row(mu[0:rw]), _row(mu[rw:rw + LANES]), _row(mu[rw + LANES:rw + lo]),
                _row(ev_w0[0]), _pad_rows(ev_w2[0], LANES, 0), _row(ev_a0[0]), _pad_rows(ev_a2[0], LANES, 64),
                ev_g2[0].astype(F32), _row(ev_k_k[0]), _row(ev_k_a[0]), _row(ev_r_k[0]), _row(ev_gn_g[0]),
                _row(ev_gn_b[0]), _head_sum_matrix(), _tril_ones(RWKV_CHUNK, RWKV_STEP_CHUNKS))
    lru_prm = (ev_lru_conv_w[0], _row(ev_lru_conv_b[0]), _block_diag(ev_lru_wa[0]).astype(BF16), _row(ev_lru_ba[0]),
               _block_diag(ev_lru_wx[0]).astype(BF16), _row(ev_lru_bx[0]), _row(ev_lru_lam[0]))
    y_a, y_b = _mix0(p, rwkv_prm, lru_prm, bsz, seq)
    w_out = ev_w_out[0].astype(BF16)
    h = _mix_out(y_a, y_b, w_out[0:RWKV_WIDTH], w_out[RWKV_WIDTH:], h, _row(ln_mix_g[0]), _row(ln_mix_b[0]), 512)
    h = _sublayers(h, mem2, 0, bsz, *xa)

    w_in = jnp.pad(od_w_in[0], ((0, 0), (0, ODD_IN_PAD - ODD_IN))).astype(BF16)
    p = _proj(h, w_in, 256)
    conf_prm = (od_cf_conv_w[0], _row(od_cf_conv_b[0]), _row(od_cf_ln_g[0]), _row(od_cf_ln_b[0]))
    y_c = _conf(p, conf_prm, bsz, seq)
    cw, cb = od_ssm_conv_w[0], od_ssm_conv_b[0]
    pad16 = lambda v: jnp.pad(v, (0, LANES - SSM_HEADS)).reshape(1, LANES).astype(F32)
    ssd_prm = (cw[:, 0:SSM_INNER], _row(cb[0:SSM_INNER]), cw[:, SSM_INNER:], _row(cb[SSM_INNER:]),
               pad16(od_dt_bias[0]), pad16(od_a_log[0]), _row(jnp.repeat(od_d_skip[0], SSM_HEAD_DIM)),
               _row(od_ssm_norm_g[0]), _head_expand_matrix(), _tril_ones(SSM_CHUNK))
    y_d = _ssd(p, ssd_prm, bsz, seq)
    w_out = od_w_out[0].astype(BF16)
    h = _mix_out(y_c, y_d, w_out[0:CONF_WIDTH], w_out[CONF_WIDTH:], h, _row(ln_mix_g[1]), _row(ln_mix_b[1]), 512)
    h = _sublayers(h, mem2, 1, bsz, *xa)
    return h.reshape(bsz, seq, D_MODEL)
```

```python
import jax
import jax.numpy as jnp
import numpy as np
from jax import lax
from jax.experimental import pallas as pl
from jax.experimental.pallas import tpu as pltpu

F32 = jnp.float32
BF16 = jnp.bfloat16

D_MODEL = 1024
DEPTH = 2
DN_ALPHA = (2 * DEPTH) ** 0.25
LN_EPS = 1e-5

RWKV_WIDTH = 512
RWKV_HEAD_DIM = 64
RWKV_HEADS = 8
RWKV_GN_EPS = 64e-5
LOG2_E = 1.4426950408889634
HALF_DECAY = 0.6065306597126334
RWKV_CHUNK = 64
RWKV_STEP_CHUNKS = 4
RWKV_GROUP = 4

LRU_WIDTH = 512
LRU_C = 8.0
LRU_CONV = 4
LRU_TILE = 256

CONF_WIDTH = 512
CONF_CONV = 31
CONF_HALO = 32
CONF_TILE = 256

SSM_INNER = 1024
SSM_HEAD_DIM = 64
SSM_HEADS = 16
SSM_GROUPS = 2
SSM_STATE = 128
SSM_CONV = 4
SSM_CHUNK = 128
SSM_STEP_CHUNKS = 2
ODD_IN = 3600
ODD_IN_PAD = 3712

EPILOGUE_SPLIT = 2
XA_HEADS = 4
XA_HEAD_DIM = 256
D_FF = 4096

SUBLANES = 8
LANES = 128
VMEM_CAP = 56 * 1024 * 1024


def _cparams(semantics, vmem_bytes):
    return pltpu.CompilerParams(dimension_semantics=semantics,
                                vmem_limit_bytes=int(min(max(vmem_bytes, 16 * 1024 * 1024), VMEM_CAP)))


def _dot(a, b):
    return jnp.dot(a.astype(BF16), b.astype(BF16), preferred_element_type=F32)


def _split_parts(x, parts):
    out = []
    for _ in range(parts - 1):
        hi = x.astype(BF16)
        out.append(hi)
        x = x - hi.astype(F32)
    out.append(x.astype(BF16))
    return out


def _dot_onehot_rhs(x, m, parts):
    acc = None
    for xp in _split_parts(x, parts):
        t = jnp.dot(xp, m, preferred_element_type=F32)
        acc = t if acc is None else acc + t
    return acc


def _dot_onehot_lhs(m, x, parts):
    acc = None
    for xp in _split_parts(x, parts):
        t = jnp.dot(m, xp, preferred_element_type=F32)
        acc = t if acc is None else acc + t
    return acc


def _head_sums(terms, bdm):
    pieces, owner = [], []
    for n, (x, parts) in enumerate(terms):
        for p in _split_parts(x, parts):
            pieces.append(p)
            owner.append(n)
    rows = terms[0][0].shape[0]
    w = bdm.shape[0]
    stacked = jnp.concatenate(pieces, axis=0)
    sums = jnp.concatenate([jnp.dot(stacked[:, q * w:(q + 1) * w], bdm, preferred_element_type=F32)
                            for q in range(stacked.shape[1] // w)], axis=1)
    out = [None] * len(terms)
    for j, n in enumerate(owner):
        piece = sums[j * rows:(j + 1) * rows]
        out[n] = piece if out[n] is None else out[n] + piece
    return out


def _dot_nt(a, b):
    return lax.dot_general(a.astype(BF16), b.astype(BF16), (((1,), (1,)), ((), ())),
                           preferred_element_type=F32)


def _sigmoid(x):
    return 1.0 / (1.0 + jnp.exp(-x))


def _softplus(x):
    return jnp.maximum(x, 0.0) + jnp.log(1.0 + jnp.exp(-jnp.abs(x)))


def _silu(x):
    return x * _sigmoid(x)


def _gelu_tanh(x):
    return 0.5 * x * (1.0 + jnp.tanh(0.7978845608028654 * (x + 0.044715 * (x * x * x))))


def _res_ln(h, y, g, b):
    z = DN_ALPHA * h + y
    mu = jnp.mean(z, axis=-1, keepdims=True)
    zc = z - mu
    var = jnp.mean(zc * zc, axis=-1, keepdims=True)
    return zc * lax.rsqrt(var + LN_EPS) * g + b


def _proj_kernel(x_ref, w_ref, o_ref):
    o_ref[...] = _dot(x_ref[...], w_ref[...]).astype(o_ref.dtype)


def _proj(x, w, tm, out_dtype=F32):
    n, k = x.shape
    m = w.shape[1]
    vmem = 2 * (tm * k * x.dtype.itemsize + tm * m * jnp.dtype(out_dtype).itemsize) + k * m * 2 + (4 << 20)
    return pl.pallas_call(
        _proj_kernel,
        grid=(n // tm,),
        in_specs=[pl.BlockSpec((tm, k), lambda i: (i, 0)),
                  pl.BlockSpec((k, m), lambda i: (0, 0), pipeline_mode=pl.Buffered(1))],
        out_specs=pl.BlockSpec((tm, m), lambda i: (i, 0)),
        out_shape=jax.ShapeDtypeStruct((n, m), out_dtype),
        compiler_params=_cparams(("parallel",), vmem),
    )(x, w)


def _mix_out_kernel(a_ref, b_ref, wa_ref, wb_ref, h_ref, g_ref, bias_ref, o_ref):
    rows = o_ref.shape[0] // EPILOGUE_SPLIT
    ys = []
    for s in range(EPILOGUE_SPLIT):
        sl = slice(s * rows, (s + 1) * rows)
        ys.append(_dot(a_ref[sl, :], wa_ref[...]) + _dot(b_ref[sl, :], wb_ref[...]))
    for s in range(EPILOGUE_SPLIT):
        sl = slice(s * rows, (s + 1) * rows)
        o_ref[sl, :] = _res_ln(h_ref[sl, :], ys[s], g_ref[...], bias_ref[...])


def _mix_out(a, b, wa, wb, h, g, bias, tm):
    n = h.shape[0]
    ka, kb = a.shape[1], b.shape[1]
    vmem = 2 * (tm * (ka + kb) * 2 + 2 * tm * D_MODEL * 4 + (ka + kb) * D_MODEL * 2) + (8 << 20)
    row = lambda i: (i, 0)
    fix = lambda i: (0, 0)
    return pl.pallas_call(
        _mix_out_kernel,
        grid=(n // tm,),
        in_specs=[pl.BlockSpec((tm, ka), row), pl.BlockSpec((tm, kb), row),
                  pl.BlockSpec((ka, D_MODEL), fix), pl.BlockSpec((kb, D_MODEL), fix),
                  pl.BlockSpec((tm, D_MODEL), row), pl.BlockSpec((1, D_MODEL), fix),
                  pl.BlockSpec((1, D_MODEL), fix)],
        out_specs=pl.BlockSpec((tm, D_MODEL), row),
        out_shape=jax.ShapeDtypeStruct((n, D_MODEL), F32),
        compiler_params=_cparams(("parallel",), vmem),
    )(a, b, wa, wb, h, g, bias)


def _attn_kernel(h_ref, k_ref, v_ref, wq_ref, wo_ref, g_ref, b_ref, o_ref, cat_ref):
    rows = o_ref.shape[0] // EPILOGUE_SPLIT
    subs = [slice(s * rows, (s + 1) * rows) for s in range(EPILOGUE_SPLIT)]
    q = [_dot(h_ref[rs, :], wq_ref[...]) for rs in subs]
    for hd in range(XA_HEADS):
        sl = slice(hd * XA_HEAD_DIM, (hd + 1) * XA_HEAD_DIM)
        for n, rs in enumerate(subs):
            s = _dot_nt(q[n][:, sl], k_ref[:, sl]) * (XA_HEAD_DIM ** -0.5)
            s = s - jnp.max(s, axis=-1, keepdims=True)
            e = jnp.exp(s)
            p = e / jnp.sum(e, axis=-1, keepdims=True)
            cat_ref[rs, sl] = _dot(p, v_ref[:, sl])
    y = [_dot(cat_ref[rs, :], wo_ref[...]) for rs in subs]
    for n, rs in enumerate(subs):
        o_ref[rs, :] = _res_ln(h_ref[rs, :], y[n], g_ref[...], b_ref[...])


def _attn(h, kmem, vmem_, wq, wo, g, b, bsz, tm):
    n = h.shape[0]
    tiles = n // bsz // tm
    n_mem = kmem.shape[0] // bsz
    vmem = 2 * (2 * tm * D_MODEL * 4 + 2 * n_mem * D_MODEL * 2 + 2 * D_MODEL * D_MODEL * 2) + tm * D_MODEL * 16 + (8 << 20)
    row = lambda bi, i: (bi * tiles + i, 0)
    mem = lambda bi, i: (bi, 0)
    fix = lambda bi, i: (0, 0)
    return pl.pallas_call(
        _attn_kernel,
        grid=(bsz, tiles),
        in_specs=[pl.BlockSpec((tm, D_MODEL), row), pl.BlockSpec((n_mem, D_MODEL), mem),
                  pl.BlockSpec((n_mem, D_MODEL), mem), pl.BlockSpec((D_MODEL, D_MODEL), fix),
                  pl.BlockSpec((D_MODEL, D_MODEL), fix), pl.BlockSpec((1, D_MODEL), fix),
                  pl.BlockSpec((1, D_MODEL), fix)],
        out_specs=pl.BlockSpec((tm, D_MODEL), row),
        out_shape=jax.ShapeDtypeStruct((n, D_MODEL), F32),
        scratch_shapes=[pltpu.VMEM((tm, D_MODEL), F32)],
        compiler_params=_cparams(("parallel", "parallel"), vmem),
    )(h, kmem, vmem_, wq, wo, g, b)


def _ffn_kernel(h_ref, w1_ref, w2_ref, g_ref, b_ref, o_ref):
    rows = o_ref.shape[0] // EPILOGUE_SPLIT
    subs = [slice(s * rows, (s + 1) * rows) for s in range(EPILOGUE_SPLIT)]
    u = [jnp.square(jnp.maximum(_dot(h_ref[rs, :], w1_ref[...]), 0.0)).astype(BF16) for rs in subs]
    y = [_dot(u[n], w2_ref[...]) for n in range(EPILOGUE_SPLIT)]
    for n, rs in enumerate(subs):
        o_ref[rs, :] = _res_ln(h_ref[rs, :], y[n], g_ref[...], b_ref[...])


def _ffn(h, w1, w2, g, b, tm):
    n = h.shape[0]
    vmem = 2 * D_MODEL * D_FF * 2 + 4 * tm * D_MODEL * 4 + tm * D_FF * 6 + tm * D_MODEL * 6 + (6 << 20)
    fix = lambda i: (0, 0)
    return pl.pallas_call(
        _ffn_kernel,
        grid=(n // tm,),
        in_specs=[pl.BlockSpec((tm, D_MODEL), lambda i: (i, 0)),
                  pl.BlockSpec((D_MODEL, D_FF), fix, pipeline_mode=pl.Buffered(1)),
                  pl.BlockSpec((D_FF, D_MODEL), fix, pipeline_mode=pl.Buffered(1)),
                  pl.BlockSpec((1, D_MODEL), fix),
                  pl.BlockSpec((1, D_MODEL), fix)],
        out_specs=pl.BlockSpec((tm, D_MODEL), lambda i: (i, 0)),
        out_shape=jax.ShapeDtypeStruct((n, D_MODEL), F32),
        compiler_params=_cparams(("parallel",), vmem),
    )(h, w1, w2, g, b)


def _shifted(ext_ref, x, first, rows):
    if first is not None:
        @pl.when(first)
        def _():
            ext_ref[...] = jnp.zeros(ext_ref.shape, F32)

    ext = jnp.concatenate([ext_ref[...], x], axis=0)
    prev = pltpu.roll(ext, 1, axis=0)[SUBLANES:SUBLANES + rows, :]
    ext_ref[...] = x[rows - SUBLANES:rows, :]
    return prev


def _causal_conv(ext_ref, x, w_ref, b_ref, first, rows, taps, halo):
    if first is not None:
        @pl.when(first)
        def _():
            ext_ref[...] = jnp.zeros(ext_ref.shape, F32)

    ext = jnp.concatenate([ext_ref[...], x], axis=0)
    acc = x * w_ref[taps - 1:taps, :] + b_ref[...]
    for b in range(min(SUBLANES, taps)):
        shifted = ext if b == 0 else pltpu.roll(ext, b, axis=0)
        for a in range((taps - 1 - b) // SUBLANES + 1):
            lag = SUBLANES * a + b
            if lag > 0:
                start = halo - SUBLANES * a
                acc = acc + shifted[start:start + rows, :] * w_ref[taps - 1 - lag:taps - lag, :]
    ext_ref[...] = x[rows - halo:rows, :]
    return acc


def _lru_stages(xb_ref, gb_ref, cw_ref, cb_ref, wa_ref, ba_ref, wx_ref, bx_ref, lam_ref, o_ref, ext_ref, carry_ref):
    rows = LRU_TILE
    xc = _causal_conv(ext_ref, xb_ref[...], cw_ref, cb_ref, None, rows, LRU_CONV, SUBLANES)
    yield
    gate_r = _sigmoid(_dot(xc, wa_ref[...]) + ba_ref[...])
    gate_i = _sigmoid(_dot(xc, wx_ref[...]) + bx_ref[...])
    yield
    log_a = -LRU_C * _softplus(-lam_ref[...]) * gate_r
    a = jnp.exp(log_a)
    u = jnp.sqrt(1.0 - a * a) * (gate_i * xc)
    yield
    ridx = lax.broadcasted_iota(jnp.int32, (rows, LRU_WIDTH), 0)
    s = 1
    while s < SUBLANES:
        keep = ridx >= s
        a_sh = jnp.where(keep, pltpu.roll(a, s, axis=0), 1.0)
        u_sh = jnp.where(keep, pltpu.roll(u, s, axis=0), 0.0)
        u = a * u_sh + u
        a = a * a_sh
        s *= 2
        yield
    while s < rows:
        u = jnp.concatenate([u[0:s], a[s:rows] * u[0:rows - s] + u[s:rows]], axis=0)
        a = jnp.concatenate([a[0:s], a[s:rows] * a[0:rows - s]], axis=0)
        s *= 2
        yield
    h = a * carry_ref[0:1, :] + u
    carry_ref[...] = jnp.broadcast_to(h[rows - 1:rows, :], carry_ref.shape)
    o_ref[...] = (h * _gelu_tanh(gb_ref[...])).astype(o_ref.dtype)


def _mix0_kernel(rkv_ref, wa_ref, gd_ref, mu_rkv_ref, mu_wa_ref, mu_gd_ref, w0_ref, w2_ref, a0_ref, a2_ref,
                 g2_ref, kk_ref, ka_ref, rk_ref, gng_ref, gnb_ref, bdm_ref, tril_ref,
                 xb_ref, gb_ref, cw_ref, cb_ref, lwa_ref, lba_ref, lwx_ref, lbx_ref, lam_ref,
                 o_ref, o_lru_ref,
                 ext_rkv, ext_wa, ext_gd, state_ref, obuf_ref, ext_lru, carry_lru):
    L = RWKV_CHUNK
    CH = RWKV_STEP_CHUNKS
    rows = L * CH
    N = RWKV_HEAD_DIM

    @pl.when(pl.program_id(1) == 0)
    def _():
        state_ref[...] = jnp.zeros(state_ref.shape, F32)
        carry_lru[...] = jnp.zeros(carry_lru.shape, F32)
        for ext in (ext_rkv, ext_wa, ext_gd, ext_lru):
            ext[...] = jnp.zeros(ext.shape, F32)

    lru = _lru_stages(xb_ref, gb_ref, cw_ref, cb_ref, lwa_ref, lba_ref, lwx_ref, lbx_ref, lam_ref, o_lru_ref,
                      ext_lru, carry_lru)
    tick = lambda: next(lru, None)

    x = rkv_ref[...]
    xm = x + (_shifted(ext_rkv, x, None, rows) - x) * mu_rkv_ref[...]
    wa = wa_ref[...]
    wam = wa + (_shifted(ext_wa, wa, None, rows) - wa) * mu_wa_ref[...]
    gd = gd_ref[...]
    gdm = gd + (_shifted(ext_gd, gd, None, rows) - gd) * mu_gd_ref[...]

    r = xm[:, 0:RWKV_WIDTH]
    k = xm[:, RWKV_WIDTH:2 * RWKV_WIDTH]
    v = xm[:, 2 * RWKV_WIDTH:3 * RWKV_WIDTH]
    z = w0_ref[...] + _dot(jnp.tanh(wam), w2_ref[...])
    logdec = (-LOG2_E * HALF_DECAY) * _sigmoid(z)
    a = _sigmoid(a0_ref[...] + _dot(wam, a2_ref[...]))
    g = _dot(_sigmoid(gdm), g2_ref[...])
    bdm = bdm_ref[...]
    kks = k * kk_ref[...]
    k2 = k * (1.0 + (a - 1.0) * ka_ref[...])
    kk_sq, bonus_s = _head_sums([(kks * kks, 2), (r * k2 * rk_ref[...], 1)], bdm)
    kk = kks * lax.rsqrt(jnp.maximum(kk_sq, 1e-24))
    bonus = bonus_s * v
    beta = kk * a

    cum = _dot_onehot_lhs(tril_ref[...], logdec, 2)
    cum_l = jnp.concatenate(
        [jnp.broadcast_to(cum[c * L + L - 1:c * L + L, :], (L, RWKV_WIDTH)) for c in range(CH)], axis=0)
    e_to_end = jnp.exp2(cum_l - cum)
    e_neg = jnp.exp2(-cum)
    a_t = -kk * jnp.exp2(cum - logdec)
    r_t = r * jnp.exp2(cum)
    b_t = beta * e_neg
    k_t = k2 * e_neg
    b_p = beta * e_to_end
    k_p = k2 * e_to_end
    p_l = jnp.exp2(cum_l)

    W = RWKV_GROUP * N
    bdm32 = bdm.astype(F32)
    lane_s = jnp.bitwise_and(lax.broadcasted_iota(jnp.int32, (L, W), 1), N - 1)
    t_idx = lax.broadcasted_iota(jnp.int32, (L, W), 0)
    strict = lane_s < t_idx
    incl = lane_s <= t_idx
    eye_cat = jnp.where(lane_s == t_idx, 1.0, 0.0)
    mask2 = jnp.concatenate([strict, incl], axis=0)

    per_half = LANES // N
    half_mask = [bdm[j * N:j * N + L, 0:LANES] for j in range(per_half)]
    zero_half = jnp.zeros((L, LANES), BF16)

    def bd(y):
        yb = y.astype(BF16)
        blocks = []
        for h in range(RWKV_GROUP):
            half, j = divmod(h, per_half)
            part = yb[:, half * LANES:(half + 1) * LANES] * half_mask[j]
            blocks.append(jnp.concatenate([part if c == half else zero_half for c in range(W // LANES)], axis=1))
        return jnp.concatenate(blocks, axis=0)

    mmb = lambda x, y: jnp.dot(x.astype(BF16), y, preferred_element_type=F32)
    mmb_nt = lambda x, y: lax.dot_general(x.astype(BF16), y, (((1,), (1,)), ((), ())), preferred_element_type=F32)

    items = [(c, q) for c in range(CH) for q in range(RWKV_WIDTH // W)]
    cut = lambda arr, c, q: arr[c * L:(c + 1) * L, q * W:(q + 1) * W]
    a_c = {i: cut(a_t, *i) for i in items}
    r_c = {i: cut(r_t, *i) for i in items}
    v_c = {i: cut(v, *i) for i in items}
    bp_c = {i: cut(b_p, *i) for i in items}
    lhs = {i: jnp.concatenate([a_c[i], r_c[i]], axis=0).astype(BF16) for i in items}
    q_b = {i: mmb_nt(lhs[i], bd(cut(b_t, *i))) for i in items}
    tick()
    q_k = {i: jnp.where(mask2, mmb_nt(lhs[i], bd(cut(k_t, *i))), 0.0) for i in items}
    tick()
    qkv = {i: mmb(q_k[i], bd(v_c[i])) for i in items}
    tick()
    pw = {i: jnp.where(strict, q_b[i][0:L], 0.0) for i in items}
    tinv = {i: eye_cat + pw[i] for i in items}
    pw = {i: mmb(pw[i], bd(pw[i])) for i in items}
    tick()
    for _ in range(L.bit_length() - 3):
        both = {i: mmb(jnp.concatenate([pw[i], tinv[i]], axis=0), bd(pw[i])) for i in items}
        pw = {i: both[i][0:L] for i in items}
        tinv = {i: tinv[i] + both[i][L:2 * L] for i in items}
        tick()
    tinv = {i: tinv[i] + mmb(tinv[i], bd(pw[i])) for i in items}
    tick()
    ta = {i: mmb(tinv[i], bd(a_c[i])) for i in items}
    w2 = {i: mmb(tinv[i], bd(qkv[i][0:L])) for i in items}
    tick()
    a_rb = {i: jnp.where(incl, q_b[i][L:2 * L], 0.0).astype(BF16) for i in items}
    q_t = {i: r_c[i] + mmb(a_rb[i], bd(ta[i])) for i in items}
    o_loc = {i: mmb(a_rb[i], bd(w2[i])) + qkv[i][L:2 * L] for i in items}
    tick()
    gk = {i: _dot(ta[i].T, bp_c[i]).astype(BF16) * bdm for i in items}
    s_loc = {i: _dot(jnp.concatenate([w2[i], v_c[i]], axis=0).T,
                     jnp.concatenate([bp_c[i], cut(k_p, *i)], axis=0)) * bdm32 for i in items}
    tick()
    state = [state_ref[q] for q in range(RWKV_WIDTH // W)]
    for c in range(CH):
        for q in range(RWKV_WIDTH // W):
            i = (c, q)
            obuf_ref[c * L:(c + 1) * L, q * W:(q + 1) * W] = mmb_nt(q_t[i], state[q].astype(BF16)) + o_loc[i]
            state[q] = state[q] * p_l[c * L:c * L + 1, q * W:(q + 1) * W] + mmb(state[q], gk[i]) + s_loc[i]
        tick()
    for q in range(RWKV_WIDTH // W):
        state_ref[q] = state[q]

    o = obuf_ref[...]
    oc = o - _head_sums([(o, 1)], bdm)[0] * (1.0 / N)
    var = _head_sums([(oc * oc, 1)], bdm)[0] * (1.0 / N)
    o_ref[...] = ((oc * lax.rsqrt(var + RWKV_GN_EPS) * gng_ref[...] + gnb_ref[...] + bonus) * g).astype(o_ref.dtype)
    for _ in lru:
        pass


def _mix0(p, rwkv_prm, lru_prm, bsz, seq):
    L = RWKV_CHUNK * RWKV_STEP_CHUNKS
    assert L == LRU_TILE
    tiles = seq // L
    n = bsz * seq
    gw = RWKV_GROUP * RWKV_HEAD_DIM
    row = lambda c: (lambda bi, i: (bi * tiles + i, c))
    fix = lambda bi, i: (0, 0)
    vec = lambda width: pl.BlockSpec((1, width), fix)
    mat = lambda r, c: pl.BlockSpec((r, c), fix)
    in_specs = [pl.BlockSpec((L, 3 * RWKV_WIDTH), row(0)),
                pl.BlockSpec((L, LANES), row(20)), pl.BlockSpec((L, LANES), row(21)),
                vec(3 * RWKV_WIDTH), vec(LANES), vec(LANES),
                vec(RWKV_WIDTH), mat(LANES, RWKV_WIDTH), vec(RWKV_WIDTH), mat(LANES, RWKV_WIDTH),
                mat(LANES, RWKV_WIDTH),
                vec(RWKV_WIDTH), vec(RWKV_WIDTH), vec(RWKV_WIDTH), vec(RWKV_WIDTH), vec(RWKV_WIDTH),
                mat(gw, gw), mat(L, L),
                pl.BlockSpec((L, LRU_WIDTH), row(3)), pl.BlockSpec((L, LRU_WIDTH), row(4)),
                mat(LRU_CONV, LRU_WIDTH), vec(LRU_WIDTH), mat(LRU_WIDTH, LRU_WIDTH), vec(LRU_WIDTH),
                mat(LRU_WIDTH, LRU_WIDTH), vec(LRU_WIDTH), vec(LRU_WIDTH)]
    out_row = lambda bi, i: (bi * tiles + i, 0)
    return pl.pallas_call(
        _mix0_kernel,
        grid=(bsz, tiles),
        in_specs=in_specs,
        out_specs=[pl.BlockSpec((L, RWKV_WIDTH), out_row), pl.BlockSpec((L, LRU_WIDTH), out_row)],
        out_shape=[jax.ShapeDtypeStruct((n, RWKV_WIDTH), BF16), jax.ShapeDtypeStruct((n, LRU_WIDTH), BF16)],
        scratch_shapes=[pltpu.VMEM((SUBLANES, 3 * RWKV_WIDTH), F32),
                        pltpu.VMEM((SUBLANES, LANES), F32),
                        pltpu.VMEM((SUBLANES, LANES), F32),
                        pltpu.VMEM((RWKV_HEADS // RWKV_GROUP, gw, gw), F32),
                        pltpu.VMEM((L, RWKV_WIDTH), F32),
                        pltpu.VMEM((SUBLANES, LRU_WIDTH), F32),
                        pltpu.VMEM((SUBLANES, LRU_WIDTH), F32)],
        compiler_params=_cparams(("arbitrary", "arbitrary"), 48 << 20),
    )(p, p, p, *rwkv_prm, p, p, *lru_prm)


def _conf_kernel(c_ref, cw_ref, cb_ref, g_ref, b_ref, o_ref, ext_ref):
    rows = CONF_TILE
    first = pl.program_id(1) == 0
    c = c_ref[...]
    glu = c[:, 0:CONF_WIDTH] * _sigmoid(c[:, CONF_WIDTH:2 * CONF_WIDTH])
    u = _causal_conv(ext_ref, glu, cw_ref, cb_ref, first, rows, CONF_CONV, CONF_HALO)
    mu = jnp.mean(u, axis=-1, keepdims=True)
    uc = u - mu
    var = jnp.mean(uc * uc, axis=-1, keepdims=True)
    o_ref[...] = _silu(uc * lax.rsqrt(var + LN_EPS) * g_ref[...] + b_ref[...]).astype(o_ref.dtype)


def _conf(p, prm, bsz, seq):
    rows = CONF_TILE
    tiles = seq // rows
    n = bsz * seq
    fix = lambda bi, i: (0, 0)
    vec = pl.BlockSpec((1, CONF_WIDTH), fix)
    return pl.pallas_call(
        _conf_kernel,
        grid=(bsz, tiles),
        in_specs=[pl.BlockSpec((rows, 2 * CONF_WIDTH), lambda bi, i: (bi * tiles + i, 0)),
                  pl.BlockSpec((CONF_CONV, CONF_WIDTH), fix), vec, vec, vec],
        out_specs=pl.BlockSpec((rows, CONF_WIDTH), lambda bi, i: (bi * tiles + i, 0)),
        out_shape=jax.ShapeDtypeStruct((n, CONF_WIDTH), BF16),
        scratch_shapes=[pltpu.VMEM((CONF_HALO, CONF_WIDTH), F32)],
        compiler_params=_cparams(("arbitrary", "arbitrary"), 32 << 20),
    )(p, *prm)


def _ssd_kernel(z_ref, xs_ref, bc_ref, dt_ref, cwx_ref, cbx_ref, cwb_ref, cbb_ref, dtb_ref, alog_ref,
                dskip_ref, normg_ref, expand_ref, tril_ref, o_ref, ext_xs, ext_bc, state_ref, ybuf_ref):
    L = SSM_CHUNK
    CH = SSM_STEP_CHUNKS
    rows = L * CH
    P = SSM_HEAD_DIM
    NS = SSM_STATE
    HG = SSM_HEADS // SSM_GROUPS
    first = pl.program_id(1) == 0

    @pl.when(first)
    def _():
        state_ref[...] = jnp.zeros(state_ref.shape, F32)

    xs = _silu(_causal_conv(ext_xs, xs_ref[...], cwx_ref, cbx_ref, first, rows, SSM_CONV, SUBLANES))
    bc = _silu(_causal_conv(ext_bc, bc_ref[...], cwb_ref, cbb_ref, first, rows, SSM_CONV, SUBLANES))
    dt = _softplus(dt_ref[...] + dtb_ref[...])
    a_head = -jnp.exp(alog_ref[...])
    acum = _dot_onehot_lhs(tril_ref[...], dt * (a_head * LOG2_E), 3)
    acum_x = _dot_onehot_rhs(acum, expand_ref[...], 3)
    acum_xl = jnp.concatenate(
        [jnp.broadcast_to(acum_x[c * L + L - 1:c * L + L, :], (L, SSM_INNER)) for c in range(CH)], axis=0)
    xdt = xs * _dot_onehot_rhs(dt, expand_ref[...], 3)
    x_end = (xdt * jnp.exp2(acum_xl - acum_x)).astype(BF16)
    from_start = jnp.exp2(acum_x)
    chunk_decay = jnp.exp2(acum_xl)
    causal = lax.broadcasted_iota(jnp.int32, (L, L), 1) <= lax.broadcasted_iota(jnp.int32, (L, L), 0)

    half = SSM_INNER // SSM_GROUPS
    state = state_ref[...]
    cb = {}
    for c in range(CH):
        cs = slice(c * L, (c + 1) * L)
        new_state = []
        for g in range(SSM_GROUPS):
            gs = slice(g * half, (g + 1) * half)
            bm = bc[cs, g * NS:(g + 1) * NS]
            cm = bc[cs, (SSM_GROUPS + g) * NS:(SSM_GROUPS + g + 1) * NS].astype(BF16)
            cb[c, g] = _dot_nt(cm, bm)
            ybuf_ref[cs, gs] = _dot(cm, state[:, gs]) * from_start[cs, gs]
            new_state.append(state[:, gs] * chunk_decay[c * L:c * L + 1, gs] + _dot(bm.T, x_end[cs, gs]))
        state = jnp.concatenate(new_state, axis=1)
    state_ref[...] = state
    y_off = ybuf_ref[...]
    for c in range(CH):
        cs = slice(c * L, (c + 1) * L)
        acum_c = acum[cs, :]
        acum_t = acum_c.T
        for h in range(SSM_HEADS):
            decay = jnp.exp2(jnp.where(causal, acum_c[:, h:h + 1] - acum_t[h:h + 1, :], -jnp.inf))
            ybuf_ref[cs, h * P:(h + 1) * P] = _dot(cb[c, h // HG] * decay, xdt[cs, h * P:(h + 1) * P])

    y = (ybuf_ref[...] + y_off + dskip_ref[...] * xs) * _silu(z_ref[...])
    for g in range(SSM_GROUPS):
        yg = y[:, g * half:(g + 1) * half]
        ms = jnp.mean(yg * yg, axis=-1, keepdims=True)
        o_ref[:, g * half:(g + 1) * half] = (yg * lax.rsqrt(ms + LN_EPS)
                                             * normg_ref[:, g * half:(g + 1) * half]).astype(o_ref.dtype)


def _ssd(p, prm, bsz, seq):
    L = SSM_CHUNK * SSM_STEP_CHUNKS
    tiles = seq // L
    n = bsz * seq
    row = lambda c: (lambda bi, i: (bi * tiles + i, c))
    fix = lambda bi, i: (0, 0)
    bcw = 2 * SSM_GROUPS * SSM_STATE
    vec = lambda width: pl.BlockSpec((1, width), fix)
    in_specs = [pl.BlockSpec((L, SSM_INNER), row(1)), pl.BlockSpec((L, SSM_INNER), row(2)),
                pl.BlockSpec((L, bcw), row(6)), pl.BlockSpec((L, LANES), row(28)),
                pl.BlockSpec((SSM_CONV, SSM_INNER), fix), vec(SSM_INNER),
                pl.BlockSpec((SSM_CONV, bcw), fix), vec(bcw),
                vec(LANES), vec(LANES), vec(SSM_INNER), vec(SSM_INNER),
                pl.BlockSpec((LANES, SSM_INNER), fix), pl.BlockSpec((L, L), fix)]
    return pl.pallas_call(
        _ssd_kernel,
        grid=(bsz, tiles),
        in_specs=in_specs,
        out_specs=pl.BlockSpec((L, SSM_INNER), lambda bi, i: (bi * tiles + i, 0)),
        out_shape=jax.ShapeDtypeStruct((n, SSM_INNER), BF16),
        scratch_shapes=[pltpu.VMEM((SUBLANES, SSM_INNER), F32),
                        pltpu.VMEM((SUBLANES, bcw), F32),
                        pltpu.VMEM((SSM_STATE, SSM_INNER), F32),
                        pltpu.VMEM((L, SSM_INNER), F32)],
        compiler_params=_cparams(("arbitrary", "arbitrary"), 40 << 20),
    )(p, p, p, p, *prm)


def _row(v):
    return v.reshape(1, -1).astype(F32)


def _pad_rows(m, rows, offset):
    out = jnp.zeros((rows, m.shape[1]), F32)
    return out.at[offset:offset + m.shape[0]].set(m)


def _block_diag(w):
    nb, d, e = w.shape
    eye = jnp.eye(nb, dtype=w.dtype)
    return (eye[:, None, :, None] * w[:, :, None, :]).reshape(nb * d, nb * e)


def _tril_ones(n, blocks=1):
    return jnp.asarray(np.kron(np.eye(blocks, dtype=np.float32), np.tril(np.ones((n, n), np.float32))), dtype=BF16)


def _head_sum_matrix():
    idx = np.arange(RWKV_GROUP * RWKV_HEAD_DIM) // RWKV_HEAD_DIM
    return jnp.asarray((idx[:, None] == idx[None, :]).astype(np.float32), dtype=BF16)


def _head_expand_matrix():
    m = np.zeros((LANES, SSM_INNER), np.float32)
    for h in range(SSM_HEADS):
        m[h, h * SSM_HEAD_DIM:(h + 1) * SSM_HEAD_DIM] = 1.0
    return jnp.asarray(m, dtype=BF16)


def _sublayers(h, mem2, i, bsz, xa_wq, xa_wk, xa_wv, xa_wo, ffn_w1, ffn_w2, ln_mem_g, ln_mem_b, ln_ffn_g, ln_ffn_b):
    kmem = _proj(mem2, xa_wk[i].astype(BF16), mem2.shape[0] // bsz, BF16)
    vmem_ = _proj(mem2, xa_wv[i].astype(BF16), mem2.shape[0] // bsz, BF16)
    h = _attn(h, kmem, vmem_, xa_wq[i].astype(BF16), xa_wo[i].astype(BF16), _row(ln_mem_g[i]), _row(ln_mem_b[i]),
              bsz, 512)
    return _ffn(h, ffn_w1[i].astype(BF16), ffn_w2[i].astype(BF16), _row(ln_ffn_g[i]), _row(ln_ffn_b[i]), 512)


def kernel(x, mem, ev_w_in, ev_mu, ev_w0, ev_w2, ev_a0, ev_a2, ev_g2, ev_k_k, ev_k_a, ev_r_k, ev_gn_g, ev_gn_b, ev_lru_conv_w, ev_lru_conv_b, ev_lru_wa, ev_lru_ba, ev_lru_wx, ev_lru_bx, ev_lru_lam, ev_w_out, od_w_in, od_cf_conv_w, od_cf_conv_b, od_cf_ln_g, od_cf_ln_b, od_ssm_conv_w, od_ssm_conv_b, od_dt_bias, od_a_log, od_d_skip, od_ssm_norm_g, od_w_out, xa_wq, xa_wk, xa_wv, xa_wo, ffn_w1, ffn_w2, ln_mix_g, ln_mix_b, ln_mem_g, ln_mem_b, ln_ffn_g, ln_ffn_b):
    bsz, seq, _ = x.shape
    n = bsz * seq
    h = x.reshape(n, D_MODEL)
    mem2 = mem.reshape(bsz * mem.shape[1], D_MODEL)
    xa = (xa_wq, xa_wk, xa_wv, xa_wo, ffn_w1, ffn_w2, ln_mem_g, ln_mem_b, ln_ffn_g, ln_ffn_b)

    rw = 3 * RWKV_WIDTH
    lo = 2 * LANES
    w_in = jnp.concatenate([ev_w_in[0][:, 0:rw], ev_w_in[0][:, rw + lo:], ev_w_in[0][:, rw:rw + lo]], axis=1)
    p = _proj(h, w_in.astype(BF16), 512)
    mu = ev_mu[0]
    rwkv_prm = (_row(mu[0:rw]), _row(mu[rw:rw + LANES]), _row(mu[rw + LANES:rw + lo]),
                _row(ev_w0[0]), _pad_rows(ev_w2[0], LANES, 0), _row(ev_a0[0]), _pad_rows(ev_a2[0], LANES, 64),
                ev_g2[0].astype(F32), _row(ev_k_k[0]), _row(ev_k_a[0]), _row(ev_r_k[0]), _row(ev_gn_g[0]),
                _row(ev_gn_b[0]), _head_sum_matrix(), _tril_ones(RWKV_CHUNK, RWKV_STEP_CHUNKS))
    lru_prm = (ev_lru_conv_w[0], _row(ev_lru_conv_b[0]), _block_diag(ev_lru_wa[0]).astype(BF16), _row(ev_lru_ba[0]),
               _block_diag(ev_lru_wx[0]).astype(BF16), _row(ev_lru_bx[0]), _row(ev_lru_lam[0]))
    y_a, y_b = _mix0(p, rwkv_prm, lru_prm, bsz, seq)
    w_out = ev_w_out[0].astype(BF16)
    h = _mix_out(y_a, y_b, w_out[0:RWKV_WIDTH], w_out[RWKV_WIDTH:], h, _row(ln_mix_g[0]), _row(ln_mix_b[0]), 512)
    h = _sublayers(h, mem2, 0, bsz, *xa)

    w_in = jnp.pad(od_w_in[0], ((0, 0), (0, ODD_IN_PAD - ODD_IN))).astype(BF16)
    p = _proj(h, w_in, 512)
    conf_prm = (od_cf_conv_w[0], _row(od_cf_conv_b[0]), _row(od_cf_ln_g[0]), _row(od_cf_ln_b[0]))
    y_c = _conf(p, conf_prm, bsz, seq)
    cw, cb = od_ssm_conv_w[0], od_ssm_conv_b[0]
    pad16 = lambda v: jnp.pad(v, (0, LANES - SSM_HEADS)).reshape(1, LANES).astype(F32)
    ssd_prm = (cw[:, 0:SSM_INNER], _row(cb[0:SSM_INNER]), cw[:, SSM_INNER:], _row(cb[SSM_INNER:]),
               pad16(od_dt_bias[0]), pad16(od_a_log[0]), _row(jnp.repeat(od_d_skip[0], SSM_HEAD_DIM)),
               _row(od_ssm_norm_g[0]), _head_expand_matrix(), _tril_ones(SSM_CHUNK, SSM_STEP_CHUNKS))
    y_d = _ssd(p, ssd_prm, bsz, seq)
    w_out = od_w_out[0].astype(BF16)
    h = _mix_out(y_c, y_d, w_out[0:CONF_WIDTH], w_out[CONF_WIDTH:], h, _row(ln_mix_g[1]), _row(ln_mix_b[1]), 512)
    h = _sublayers(h, mem2, 1, bsz, *xa)
    return h.reshape(bsz, seq, D_MODEL)
```

```python
import jax
import jax.numpy as jnp
import numpy as np
from jax import lax
from jax.experimental import pallas as pl
from jax.experimental.pallas import tpu as pltpu

F32 = jnp.float32
BF16 = jnp.bfloat16

D_MODEL = 1024
DEPTH = 2
DN_ALPHA = (2 * DEPTH) ** 0.25
LN_EPS = 1e-5

RWKV_WIDTH = 512
RWKV_HEAD_DIM = 64
RWKV_HEADS = 8
RWKV_GN_EPS = 64e-5
LOG2_E = 1.4426950408889634
HALF_DECAY = 0.6065306597126334
RWKV_CHUNK = 64
RWKV_STEP_CHUNKS = 4
RWKV_GROUP = 4

LRU_WIDTH = 512
LRU_C = 8.0
LRU_CONV = 4
LRU_TILE = 256

CONF_WIDTH = 512
CONF_CONV = 31
CONF_HALO = 32
CONF_TILE = 256

SSM_INNER = 1024
SSM_HEAD_DIM = 64
SSM_HEADS = 16
SSM_GROUPS = 2
SSM_STATE = 128
SSM_CONV = 4
SSM_CHUNK = 128
SSM_STEP_CHUNKS = 2
ODD_IN = 3600
ODD_IN_PAD = 3712

EPILOGUE_SPLIT = 2
XA_HEADS = 4
XA_HEAD_DIM = 256
D_FF = 4096

SUBLANES = 8
LANES = 128
VMEM_CAP = 56 * 1024 * 1024


def _cparams(semantics, vmem_bytes):
    return pltpu.CompilerParams(dimension_semantics=semantics,
                                vmem_limit_bytes=int(min(max(vmem_bytes, 16 * 1024 * 1024), VMEM_CAP)))


def _dot(a, b):
    return jnp.dot(a.astype(BF16), b.astype(BF16), preferred_element_type=F32)


def _split_parts(x, parts):
    out = []
    for _ in range(parts - 1):
        hi = x.astype(BF16)
        out.append(hi)
        x = x - hi.astype(F32)
    out.append(x.astype(BF16))
    return out


def _dot_onehot_rhs(x, m, parts):
    acc = None
    for xp in _split_parts(x, parts):
        t = jnp.dot(xp, m, preferred_element_type=F32)
        acc = t if acc is None else acc + t
    return acc


def _dot_onehot_lhs(m, x, parts):
    acc = None
    for xp in _split_parts(x, parts):
        t = jnp.dot(m, xp, preferred_element_type=F32)
        acc = t if acc is None else acc + t
    return acc


def _head_sums(terms, bdm):
    pieces, owner = [], []
    for n, (x, parts) in enumerate(terms):
        for p in _split_parts(x, parts):
            pieces.append(p)
            owner.append(n)
    rows = terms[0][0].shape[0]
    w = bdm.shape[0]
    stacked = jnp.concatenate(pieces, axis=0)
    sums = jnp.concatenate([jnp.dot(stacked[:, q * w:(q + 1) * w], bdm, preferred_element_type=F32)
                            for q in range(stacked.shape[1] // w)], axis=1)
    out = [None] * len(terms)
    for j, n in enumerate(owner):
        piece = sums[j * rows:(j + 1) * rows]
        out[n] = piece if out[n] is None else out[n] + piece
    return out


def _dot_nt(a, b):
    return lax.dot_general(a.astype(BF16), b.astype(BF16), (((1,), (1,)), ((), ())),
                           preferred_element_type=F32)


def _sigmoid(x):
    return 1.0 / (1.0 + jnp.exp(-x))


def _softplus(x):
    return jnp.maximum(x, 0.0) + jnp.log(1.0 + jnp.exp(-jnp.abs(x)))


def _silu(x):
    return x * _sigmoid(x)


def _gelu_tanh(x):
    return 0.5 * x * (1.0 + jnp.tanh(0.7978845608028654 * (x + 0.044715 * (x * x * x))))


def _res_ln(h, y, g, b):
    z = DN_ALPHA * h + y
    mu = jnp.mean(z, axis=-1, keepdims=True)
    zc = z - mu
    var = jnp.mean(zc * zc, axis=-1, keepdims=True)
    return zc * lax.rsqrt(var + LN_EPS) * g + b


def _proj_kernel(x_ref, w_ref, o_ref):
    o_ref[...] = _dot(x_ref[...], w_ref[...]).astype(o_ref.dtype)


def _proj(x, w, tm, out_dtype=F32):
    n, k = x.shape
    m = w.shape[1]
    vmem = 2 * (tm * k * x.dtype.itemsize + tm * m * jnp.dtype(out_dtype).itemsize) + k * m * 2 + (4 << 20)
    return pl.pallas_call(
        _proj_kernel,
        grid=(n // tm,),
        in_specs=[pl.BlockSpec((tm, k), lambda i: (i, 0)),
                  pl.BlockSpec((k, m), lambda i: (0, 0), pipeline_mode=pl.Buffered(1))],
        out_specs=pl.BlockSpec((tm, m), lambda i: (i, 0)),
        out_shape=jax.ShapeDtypeStruct((n, m), out_dtype),
        compiler_params=_cparams(("parallel",), vmem),
    )(x, w)


def _mix_out_kernel(a_ref, b_ref, wa_ref, wb_ref, h_ref, g_ref, bias_ref, o_ref):
    rows = o_ref.shape[0] // EPILOGUE_SPLIT
    ys = []
    for s in range(EPILOGUE_SPLIT):
        sl = slice(s * rows, (s + 1) * rows)
        ys.append(_dot(a_ref[sl, :], wa_ref[...]) + _dot(b_ref[sl, :], wb_ref[...]))
    for s in range(EPILOGUE_SPLIT):
        sl = slice(s * rows, (s + 1) * rows)
        o_ref[sl, :] = _res_ln(h_ref[sl, :], ys[s], g_ref[...], bias_ref[...])


def _mix_out(a, b, wa, wb, h, g, bias, tm):
    n = h.shape[0]
    ka, kb = a.shape[1], b.shape[1]
    vmem = 2 * (tm * (ka + kb) * 2 + 2 * tm * D_MODEL * 4 + (ka + kb) * D_MODEL * 2) + (8 << 20)
    row = lambda i: (i, 0)
    fix = lambda i: (0, 0)
    return pl.pallas_call(
        _mix_out_kernel,
        grid=(n // tm,),
        in_specs=[pl.BlockSpec((tm, ka), row), pl.BlockSpec((tm, kb), row),
                  pl.BlockSpec((ka, D_MODEL), fix), pl.BlockSpec((kb, D_MODEL), fix),
                  pl.BlockSpec((tm, D_MODEL), row), pl.BlockSpec((1, D_MODEL), fix),
                  pl.BlockSpec((1, D_MODEL), fix)],
        out_specs=pl.BlockSpec((tm, D_MODEL), row),
        out_shape=jax.ShapeDtypeStruct((n, D_MODEL), F32),
        compiler_params=_cparams(("parallel",), vmem),
    )(a, b, wa, wb, h, g, bias)


def _attn_kernel(h_ref, k_ref, v_ref, wq_ref, wo_ref, g_ref, b_ref, o_ref, cat_ref):
    rows = o_ref.shape[0] // EPILOGUE_SPLIT
    subs = [slice(s * rows, (s + 1) * rows) for s in range(EPILOGUE_SPLIT)]
    q = [_dot(h_ref[rs, :], wq_ref[...]) for rs in subs]
    for hd in range(XA_HEADS):
        sl = slice(hd * XA_HEAD_DIM, (hd + 1) * XA_HEAD_DIM)
        for n, rs in enumerate(subs):
            s = _dot_nt(q[n][:, sl], k_ref[:, sl]) * (XA_HEAD_DIM ** -0.5)
            s = s - jnp.max(s, axis=-1, keepdims=True)
            e = jnp.exp(s)
            p = e / jnp.sum(e, axis=-1, keepdims=True)
            cat_ref[rs, sl] = _dot(p, v_ref[:, sl])
    y = [_dot(cat_ref[rs, :], wo_ref[...]) for rs in subs]
    for n, rs in enumerate(subs):
        o_ref[rs, :] = _res_ln(h_ref[rs, :], y[n], g_ref[...], b_ref[...])


def _attn(h, kmem, vmem_, wq, wo, g, b, bsz, tm):
    n = h.shape[0]
    tiles = n // bsz // tm
    n_mem = kmem.shape[0] // bsz
    vmem = 2 * (2 * tm * D_MODEL * 4 + 2 * n_mem * D_MODEL * 2 + 2 * D_MODEL * D_MODEL * 2) + tm * D_MODEL * 16 + (8 << 20)
    row = lambda bi, i: (bi * tiles + i, 0)
    mem = lambda bi, i: (bi, 0)
    fix = lambda bi, i: (0, 0)
    return pl.pallas_call(
        _attn_kernel,
        grid=(bsz, tiles),
        in_specs=[pl.BlockSpec((tm, D_MODEL), row), pl.BlockSpec((n_mem, D_MODEL), mem),
                  pl.BlockSpec((n_mem, D_MODEL), mem), pl.BlockSpec((D_MODEL, D_MODEL), fix),
                  pl.BlockSpec((D_MODEL, D_MODEL), fix), pl.BlockSpec((1, D_MODEL), fix),
                  pl.BlockSpec((1, D_MODEL), fix)],
        out_specs=pl.BlockSpec((tm, D_MODEL), row),
        out_shape=jax.ShapeDtypeStruct((n, D_MODEL), F32),
        scratch_shapes=[pltpu.VMEM((tm, D_MODEL), F32)],
        compiler_params=_cparams(("parallel", "parallel"), vmem),
    )(h, kmem, vmem_, wq, wo, g, b)


def _ffn_kernel(h_ref, w1_ref, w2_ref, g_ref, b_ref, o_ref):
    rows = o_ref.shape[0] // EPILOGUE_SPLIT
    subs = [slice(s * rows, (s + 1) * rows) for s in range(EPILOGUE_SPLIT)]
    u = [jnp.square(jnp.maximum(_dot(h_ref[rs, :], w1_ref[...]), 0.0)).astype(BF16) for rs in subs]
    y = [_dot(u[n], w2_ref[...]) for n in range(EPILOGUE_SPLIT)]
    for n, rs in enumerate(subs):
        o_ref[rs, :] = _res_ln(h_ref[rs, :], y[n], g_ref[...], b_ref[...])


def _ffn(h, w1, w2, g, b, tm):
    n = h.shape[0]
    vmem = 2 * D_MODEL * D_FF * 2 + 4 * tm * D_MODEL * 4 + tm * D_FF * 6 + tm * D_MODEL * 6 + (6 << 20)
    fix = lambda i: (0, 0)
    return pl.pallas_call(
        _ffn_kernel,
        grid=(n // tm,),
        in_specs=[pl.BlockSpec((tm, D_MODEL), lambda i: (i, 0)),
                  pl.BlockSpec((D_MODEL, D_FF), fix, pipeline_mode=pl.Buffered(1)),
                  pl.BlockSpec((D_FF, D_MODEL), fix, pipeline_mode=pl.Buffered(1)),
                  pl.BlockSpec((1, D_MODEL), fix),
                  pl.BlockSpec((1, D_MODEL), fix)],
        out_specs=pl.BlockSpec((tm, D_MODEL), lambda i: (i, 0)),
        out_shape=jax.ShapeDtypeStruct((n, D_MODEL), F32),
        compiler_params=_cparams(("parallel",), vmem),
    )(h, w1, w2, g, b)


def _shifted(ext_ref, x, first, rows):
    if first is not None:
        @pl.when(first)
        def _():
            ext_ref[...] = jnp.zeros(ext_ref.shape, F32)

    ext = jnp.concatenate([ext_ref[...], x], axis=0)
    prev = pltpu.roll(ext, 1, axis=0)[SUBLANES:SUBLANES + rows, :]
    ext_ref[...] = x[rows - SUBLANES:rows, :]
    return prev


def _causal_conv(ext_ref, x, w_ref, b_ref, first, rows, taps, halo):
    if first is not None:
        @pl.when(first)
        def _():
            ext_ref[...] = jnp.zeros(ext_ref.shape, F32)

    ext = jnp.concatenate([ext_ref[...], x], axis=0)
    acc = x * w_ref[taps - 1:taps, :] + b_ref[...]
    for b in range(min(SUBLANES, taps)):
        shifted = ext if b == 0 else pltpu.roll(ext, b, axis=0)
        for a in range((taps - 1 - b) // SUBLANES + 1):
            lag = SUBLANES * a + b
            if lag > 0:
                start = halo - SUBLANES * a
                acc = acc + shifted[start:start + rows, :] * w_ref[taps - 1 - lag:taps - lag, :]
    ext_ref[...] = x[rows - halo:rows, :]
    return acc


def _lru_stages(xb_ref, gb_ref, cw_ref, cb_ref, wa_ref, ba_ref, wx_ref, bx_ref, lam_ref, o_ref, ext_ref, carry_ref):
    rows = LRU_TILE
    xc = _causal_conv(ext_ref, xb_ref[...], cw_ref, cb_ref, None, rows, LRU_CONV, SUBLANES)
    yield
    gate_r = _sigmoid(_dot(xc, wa_ref[...]) + ba_ref[...])
    gate_i = _sigmoid(_dot(xc, wx_ref[...]) + bx_ref[...])
    yield
    log_a = -LRU_C * _softplus(-lam_ref[...]) * gate_r
    a = jnp.exp(log_a)
    u = jnp.sqrt(1.0 - a * a) * (gate_i * xc)
    yield
    ridx = lax.broadcasted_iota(jnp.int32, (rows, LRU_WIDTH), 0)
    s = 1
    while s < SUBLANES:
        keep = ridx >= s
        a_sh = jnp.where(keep, pltpu.roll(a, s, axis=0), 1.0)
        u_sh = jnp.where(keep, pltpu.roll(u, s, axis=0), 0.0)
        u = a * u_sh + u
        a = a * a_sh
        s *= 2
        yield
    while s < rows:
        u = jnp.concatenate([u[0:s], a[s:rows] * u[0:rows - s] + u[s:rows]], axis=0)
        a = jnp.concatenate([a[0:s], a[s:rows] * a[0:rows - s]], axis=0)
        s *= 2
        yield
    h = a * carry_ref[0:1, :] + u
    carry_ref[...] = jnp.broadcast_to(h[rows - 1:rows, :], carry_ref.shape)
    o_ref[...] = (h * _gelu_tanh(gb_ref[...])).astype(o_ref.dtype)


def _mix0_kernel(rkv_ref, wa_ref, gd_ref, mu_rkv_ref, mu_wa_ref, mu_gd_ref, w0_ref, w2_ref, a0_ref, a2_ref,
                 g2_ref, kk_ref, ka_ref, rk_ref, gng_ref, gnb_ref, bdm_ref, tril_ref,
                 xb_ref, gb_ref, cw_ref, cb_ref, lwa_ref, lba_ref, lwx_ref, lbx_ref, lam_ref,
                 o_ref, o_lru_ref,
                 ext_rkv, ext_wa, ext_gd, state_ref, obuf_ref, ext_lru, carry_lru):
    L = RWKV_CHUNK
    CH = RWKV_STEP_CHUNKS
    rows = L * CH
    N = RWKV_HEAD_DIM

    @pl.when(pl.program_id(1) == 0)
    def _():
        state_ref[...] = jnp.zeros(state_ref.shape, F32)
        carry_lru[...] = jnp.zeros(carry_lru.shape, F32)
        for ext in (ext_rkv, ext_wa, ext_gd, ext_lru):
            ext[...] = jnp.zeros(ext.shape, F32)

    lru = _lru_stages(xb_ref, gb_ref, cw_ref, cb_ref, lwa_ref, lba_ref, lwx_ref, lbx_ref, lam_ref, o_lru_ref,
                      ext_lru, carry_lru)
    tick = lambda: next(lru, None)

    x = rkv_ref[...]
    xm = x + (_shifted(ext_rkv, x, None, rows) - x) * mu_rkv_ref[...]
    wa = wa_ref[...]
    wam = wa + (_shifted(ext_wa, wa, None, rows) - wa) * mu_wa_ref[...]
    gd = gd_ref[...]
    gdm = gd + (_shifted(ext_gd, gd, None, rows) - gd) * mu_gd_ref[...]

    r = xm[:, 0:RWKV_WIDTH]
    k = xm[:, RWKV_WIDTH:2 * RWKV_WIDTH]
    v = xm[:, 2 * RWKV_WIDTH:3 * RWKV_WIDTH]
    z = w0_ref[...] + _dot(jnp.tanh(wam), w2_ref[...])
    logdec = (-LOG2_E * HALF_DECAY) * _sigmoid(z)
    a = _sigmoid(a0_ref[...] + _dot(wam, a2_ref[...]))
    g = _dot(_sigmoid(gdm), g2_ref[...])
    bdm = bdm_ref[...]
    kks = k * kk_ref[...]
    k2 = k * (1.0 + (a - 1.0) * ka_ref[...])
    kk_sq, bonus_s = _head_sums([(kks * kks, 2), (r * k2 * rk_ref[...], 1)], bdm)
    kk = kks * lax.rsqrt(jnp.maximum(kk_sq, 1e-24))
    bonus = bonus_s * v
    beta = kk * a

    cum = _dot_onehot_lhs(tril_ref[...], logdec, 2)
    cum_l = jnp.concatenate(
        [jnp.broadcast_to(cum[c * L + L - 1:c * L + L, :], (L, RWKV_WIDTH)) for c in range(CH)], axis=0)
    e_to_end = jnp.exp2(cum_l - cum)
    e_neg = jnp.exp2(-cum)
    a_t = -kk * jnp.exp2(cum - logdec)
    r_t = r * jnp.exp2(cum)
    b_t = beta * e_neg
    k_t = k2 * e_neg
    b_p = beta * e_to_end
    k_p = k2 * e_to_end
    p_l = jnp.exp2(cum_l)

    W = RWKV_GROUP * N
    bdm32 = bdm.astype(F32)
    lane_s = jnp.bitwise_and(lax.broadcasted_iota(jnp.int32, (L, W), 1), N - 1)
    t_idx = lax.broadcasted_iota(jnp.int32, (L, W), 0)
    strict = lane_s < t_idx
    incl = lane_s <= t_idx
    eye_cat = jnp.where(lane_s == t_idx, 1.0, 0.0)
    mask2 = jnp.concatenate([strict, incl], axis=0)

    per_half = LANES // N
    half_mask = [bdm[j * N:j * N + L, 0:LANES] for j in range(per_half)]
    zero_half = jnp.zeros((L, LANES), BF16)

    def bd(y):
        yb = y.astype(BF16)
        blocks = []
        for h in range(RWKV_GROUP):
            half, j = divmod(h, per_half)
            part = yb[:, half * LANES:(half + 1) * LANES] * half_mask[j]
            blocks.append(jnp.concatenate([part if c == half else zero_half for c in range(W // LANES)], axis=1))
        return jnp.concatenate(blocks, axis=0)

    mmb = lambda x, y: jnp.dot(x.astype(BF16), y, preferred_element_type=F32)
    mmb_nt = lambda x, y: lax.dot_general(x.astype(BF16), y, (((1,), (1,)), ((), ())), preferred_element_type=F32)

    items = [(c, q) for c in range(CH) for q in range(RWKV_WIDTH // W)]
    cut = lambda arr, c, q: arr[c * L:(c + 1) * L, q * W:(q + 1) * W]
    a_c = {i: cut(a_t, *i) for i in items}
    r_c = {i: cut(r_t, *i) for i in items}
    v_c = {i: cut(v, *i) for i in items}
    bp_c = {i: cut(b_p, *i) for i in items}
    lhs = {i: jnp.concatenate([a_c[i], r_c[i]], axis=0).astype(BF16) for i in items}
    q_b = {i: mmb_nt(lhs[i], bd(cut(b_t, *i))) for i in items}
    tick()
    q_k = {i: jnp.where(mask2, mmb_nt(lhs[i], bd(cut(k_t, *i))), 0.0) for i in items}
    tick()
    qkv = {i: mmb(q_k[i], bd(v_c[i])) for i in items}
    tick()
    pw = {i: jnp.where(strict, q_b[i][0:L], 0.0) for i in items}
    tinv = {i: eye_cat + pw[i] for i in items}
    pw = {i: mmb(pw[i], bd(pw[i])) for i in items}
    tick()
    for _ in range(L.bit_length() - 3):
        both = {i: mmb(jnp.concatenate([pw[i], tinv[i]], axis=0), bd(pw[i])) for i in items}
        pw = {i: both[i][0:L] for i in items}
        tinv = {i: tinv[i] + both[i][L:2 * L] for i in items}
        tick()
    tinv = {i: tinv[i] + mmb(tinv[i], bd(pw[i])) for i in items}
    tick()
    ta = {i: mmb(tinv[i], bd(a_c[i])) for i in items}
    w2 = {i: mmb(tinv[i], bd(qkv[i][0:L])) for i in items}
    tick()
    a_rb = {i: jnp.where(incl, q_b[i][L:2 * L], 0.0).astype(BF16) for i in items}
    q_t = {i: r_c[i] + mmb(a_rb[i], bd(ta[i])) for i in items}
    o_loc = {i: mmb(a_rb[i], bd(w2[i])) + qkv[i][L:2 * L] for i in items}
    tick()
    gk = {i: _dot(ta[i].T, bp_c[i]).astype(BF16) * bdm for i in items}
    s_loc = {i: _dot(jnp.concatenate([w2[i], v_c[i]], axis=0).T,
                     jnp.concatenate([bp_c[i], cut(k_p, *i)], axis=0)) * bdm32 for i in items}
    tick()
    state = [state_ref[q] for q in range(RWKV_WIDTH // W)]
    for c in range(CH):
        for q in range(RWKV_WIDTH // W):
            i = (c, q)
            obuf_ref[c * L:(c + 1) * L, q * W:(q + 1) * W] = mmb_nt(q_t[i], state[q].astype(BF16)) + o_loc[i]
            state[q] = state[q] * p_l[c * L:c * L + 1, q * W:(q + 1) * W] + mmb(state[q], gk[i]) + s_loc[i]
        tick()
    for q in range(RWKV_WIDTH // W):
        state_ref[q] = state[q]

    o = obuf_ref[...]
    oc = o - _head_sums([(o, 1)], bdm)[0] * (1.0 / N)
    var = _head_sums([(oc * oc, 1)], bdm)[0] * (1.0 / N)
    o_ref[...] = ((oc * lax.rsqrt(var + RWKV_GN_EPS) * gng_ref[...] + gnb_ref[...] + bonus) * g).astype(o_ref.dtype)
    for _ in lru:
        pass


def _mix0(p, rwkv_prm, lru_prm, bsz, seq):
    L = RWKV_CHUNK * RWKV_STEP_CHUNKS
    assert L == LRU_TILE
    tiles = seq // L
    n = bsz * seq
    gw = RWKV_GROUP * RWKV_HEAD_DIM
    row = lambda c: (lambda bi, i: (bi * tiles + i, c))
    fix = lambda bi, i: (0, 0)
    vec = lambda width: pl.BlockSpec((1, width), fix)
    mat = lambda r, c: pl.BlockSpec((r, c), fix)
    in_specs = [pl.BlockSpec((L, 3 * RWKV_WIDTH), row(0)),
                pl.BlockSpec((L, LANES), row(20)), pl.BlockSpec((L, LANES), row(21)),
                vec(3 * RWKV_WIDTH), vec(LANES), vec(LANES),
                vec(RWKV_WIDTH), mat(LANES, RWKV_WIDTH), vec(RWKV_WIDTH), mat(LANES, RWKV_WIDTH),
                mat(LANES, RWKV_WIDTH),
                vec(RWKV_WIDTH), vec(RWKV_WIDTH), vec(RWKV_WIDTH), vec(RWKV_WIDTH), vec(RWKV_WIDTH),
                mat(gw, gw), mat(L, L),
                pl.BlockSpec((L, LRU_WIDTH), row(3)), pl.BlockSpec((L, LRU_WIDTH), row(4)),
                mat(LRU_CONV, LRU_WIDTH), vec(LRU_WIDTH), mat(LRU_WIDTH, LRU_WIDTH), vec(LRU_WIDTH),
                mat(LRU_WIDTH, LRU_WIDTH), vec(LRU_WIDTH), vec(LRU_WIDTH)]
    out_row = lambda bi, i: (bi * tiles + i, 0)
    return pl.pallas_call(
        _mix0_kernel,
        grid=(bsz, tiles),
        in_specs=in_specs,
        out_specs=[pl.BlockSpec((L, RWKV_WIDTH), out_row), pl.BlockSpec((L, LRU_WIDTH), out_row)],
        out_shape=[jax.ShapeDtypeStruct((n, RWKV_WIDTH), BF16), jax.ShapeDtypeStruct((n, LRU_WIDTH), BF16)],
        scratch_shapes=[pltpu.VMEM((SUBLANES, 3 * RWKV_WIDTH), F32),
                        pltpu.VMEM((SUBLANES, LANES), F32),
                        pltpu.VMEM((SUBLANES, LANES), F32),
                        pltpu.VMEM((RWKV_HEADS // RWKV_GROUP, gw, gw), F32),
                        pltpu.VMEM((L, RWKV_WIDTH), F32),
                        pltpu.VMEM((SUBLANES, LRU_WIDTH), F32),
                        pltpu.VMEM((SUBLANES, LRU_WIDTH), F32)],
        compiler_params=_cparams(("arbitrary", "arbitrary"), 48 << 20),
    )(p, p, p, *rwkv_prm, p, p, *lru_prm)


def _conf_kernel(c_ref, cw_ref, cb_ref, g_ref, b_ref, o_ref, ext_ref):
    rows = CONF_TILE
    first = pl.program_id(1) == 0
    c = c_ref[...]
    glu = c[:, 0:CONF_WIDTH] * _sigmoid(c[:, CONF_WIDTH:2 * CONF_WIDTH])
    u = _causal_conv(ext_ref, glu, cw_ref, cb_ref, first, rows, CONF_CONV, CONF_HALO)
    mu = jnp.mean(u, axis=-1, keepdims=True)
    uc = u - mu
    var = jnp.mean(uc * uc, axis=-1, keepdims=True)
    o_ref[...] = _silu(uc * lax.rsqrt(var + LN_EPS) * g_ref[...] + b_ref[...]).astype(o_ref.dtype)


def _conf(p, prm, bsz, seq):
    rows = CONF_TILE
    tiles = seq // rows
    n = bsz * seq
    fix = lambda bi, i: (0, 0)
    vec = pl.BlockSpec((1, CONF_WIDTH), fix)
    return pl.pallas_call(
        _conf_kernel,
        grid=(bsz, tiles),
        in_specs=[pl.BlockSpec((rows, 2 * CONF_WIDTH), lambda bi, i: (bi * tiles + i, 0)),
                  pl.BlockSpec((CONF_CONV, CONF_WIDTH), fix), vec, vec, vec],
        out_specs=pl.BlockSpec((rows, CONF_WIDTH), lambda bi, i: (bi * tiles + i, 0)),
        out_shape=jax.ShapeDtypeStruct((n, CONF_WIDTH), BF16),
        scratch_shapes=[pltpu.VMEM((CONF_HALO, CONF_WIDTH), F32)],
        compiler_params=_cparams(("arbitrary", "arbitrary"), 32 << 20),
    )(p, *prm)


def _ssd_kernel(z_ref, xs_ref, bc_ref, dt_ref, cwx_ref, cbx_ref, cwb_ref, cbb_ref, dtb_ref, alog_ref,
                dskip_ref, normg_ref, expand_ref, tril_ref, o_ref, ext_xs, ext_bc, state_ref, ybuf_ref):
    L = SSM_CHUNK
    CH = SSM_STEP_CHUNKS
    rows = L * CH
    P = SSM_HEAD_DIM
    NS = SSM_STATE
    HG = SSM_HEADS // SSM_GROUPS
    first = pl.program_id(1) == 0

    @pl.when(first)
    def _():
        state_ref[...] = jnp.zeros(state_ref.shape, F32)

    xs = _silu(_causal_conv(ext_xs, xs_ref[...], cwx_ref, cbx_ref, first, rows, SSM_CONV, SUBLANES))
    bc = _silu(_causal_conv(ext_bc, bc_ref[...], cwb_ref, cbb_ref, first, rows, SSM_CONV, SUBLANES))
    dt = _softplus(dt_ref[...] + dtb_ref[...])
    a_head = -jnp.exp(alog_ref[...])
    acum = _dot_onehot_lhs(tril_ref[...], dt * (a_head * LOG2_E), 3)
    acum_x = _dot_onehot_rhs(acum, expand_ref[...], 3)
    acum_xl = jnp.concatenate(
        [jnp.broadcast_to(acum_x[c * L + L - 1:c * L + L, :], (L, SSM_INNER)) for c in range(CH)], axis=0)
    xdt = xs * _dot_onehot_rhs(dt, expand_ref[...], 3)
    x_end = (xdt * jnp.exp2(acum_xl - acum_x)).astype(BF16)
    from_start = jnp.exp2(acum_x)
    chunk_decay = jnp.exp2(acum_xl)
    causal = lax.broadcasted_iota(jnp.int32, (L, L), 1) <= lax.broadcasted_iota(jnp.int32, (L, L), 0)

    half = SSM_INNER // SSM_GROUPS
    state = state_ref[...]
    cb = {}
    for c in range(CH):
        cs = slice(c * L, (c + 1) * L)
        new_state = []
        for g in range(SSM_GROUPS):
            gs = slice(g * half, (g + 1) * half)
            bm = bc[cs, g * NS:(g + 1) * NS]
            cm = bc[cs, (SSM_GROUPS + g) * NS:(SSM_GROUPS + g + 1) * NS].astype(BF16)
            cb[c, g] = _dot_nt(cm, bm)
            ybuf_ref[cs, gs] = _dot(cm, state[:, gs]) * from_start[cs, gs]
            new_state.append(state[:, gs] * chunk_decay[c * L:c * L + 1, gs] + _dot(bm.T, x_end[cs, gs]))
        state = jnp.concatenate(new_state, axis=1)
    state_ref[...] = state
    y_off = ybuf_ref[...]
    for c in range(CH):
        cs = slice(c * L, (c + 1) * L)
        acum_c = acum[cs, :]
        acum_t = acum_c.T
        for h in range(SSM_HEADS):
            decay = jnp.exp2(jnp.where(causal, acum_c[:, h:h + 1] - acum_t[h:h + 1, :], -jnp.inf))
            ybuf_ref[cs, h * P:(h + 1) * P] = _dot(cb[c, h // HG] * decay, xdt[cs, h * P:(h + 1) * P])

    y = (ybuf_ref[...] + y_off + dskip_ref[...] * xs) * _silu(z_ref[...])
    for g in range(SSM_GROUPS):
        yg = y[:, g * half:(g + 1) * half]
        ms = jnp.mean(yg * yg, axis=-1, keepdims=True)
        o_ref[:, g * half:(g + 1) * half] = (yg * lax.rsqrt(ms + LN_EPS)
                                             * normg_ref[:, g * half:(g + 1) * half]).astype(o_ref.dtype)


def _ssd(p, prm, bsz, seq):
    L = SSM_CHUNK * SSM_STEP_CHUNKS
    tiles = seq // L
    n = bsz * seq
    row = lambda c: (lambda bi, i: (bi * tiles + i, c))
    fix = lambda bi, i: (0, 0)
    bcw = 2 * SSM_GROUPS * SSM_STATE
    vec = lambda width: pl.BlockSpec((1, width), fix)
    in_specs = [pl.BlockSpec((L, SSM_INNER), row(1)), pl.BlockSpec((L, SSM_INNER), row(2)),
                pl.BlockSpec((L, bcw), row(6)), pl.BlockSpec((L, LANES), row(28)),
                pl.BlockSpec((SSM_CONV, SSM_INNER), fix), vec(SSM_INNER),
                pl.BlockSpec((SSM_CONV, bcw), fix), vec(bcw),
                vec(LANES), vec(LANES), vec(SSM_INNER), vec(SSM_INNER),
                pl.BlockSpec((LANES, SSM_INNER), fix), pl.BlockSpec((L, L), fix)]
    return pl.pallas_call(
        _ssd_kernel,
        grid=(bsz, tiles),
        in_specs=in_specs,
        out_specs=pl.BlockSpec((L, SSM_INNER), lambda bi, i: (bi * tiles + i, 0)),
        out_shape=jax.ShapeDtypeStruct((n, SSM_INNER), BF16),
        scratch_shapes=[pltpu.VMEM((SUBLANES, SSM_INNER), F32),
                        pltpu.VMEM((SUBLANES, bcw), F32),
                        pltpu.VMEM((SSM_STATE, SSM_INNER), F32),
                        pltpu.VMEM((L, SSM_INNER), F32)],
        compiler_params=_cparams(("arbitrary", "arbitrary"), 40 << 20),
    )(p, p, p, p, *prm)


def _row(v):
    return v.reshape(1, -1).astype(F32)


def _pad_rows(m, rows, offset):
    out = jnp.zeros((rows, m.shape[1]), F32)
    return out.at[offset:offset + m.shape[0]].set(m)


def _block_diag(w):
    nb, d, e = w.shape
    eye = jnp.eye(nb, dtype=w.dtype)
    return (eye[:, None, :, None] * w[:, :, None, :]).reshape(nb * d, nb * e)


def _tril_ones(n, blocks=1):
    return jnp.asarray(np.kron(np.eye(blocks, dtype=np.float32), np.tril(np.ones((n, n), np.float32))), dtype=BF16)


def _head_sum_matrix():
    idx = np.arange(RWKV_GROUP * RWKV_HEAD_DIM) // RWKV_HEAD_DIM
    return jnp.asarray((idx[:, None] == idx[None, :]).astype(np.float32), dtype=BF16)


def _head_expand_matrix():
    m = np.zeros((LANES, SSM_INNER), np.float32)
    for h in range(SSM_HEADS):
        m[h, h * SSM_HEAD_DIM:(h + 1) * SSM_HEAD_DIM] = 1.0
    return jnp.asarray(m, dtype=BF16)


def _sublayers(h, mem2, i, bsz, xa_wq, xa_wk, xa_wv, xa_wo, ffn_w1, ffn_w2, ln_mem_g, ln_mem_b, ln_ffn_g, ln_ffn_b):
    kmem = _proj(mem2, xa_wk[i].astype(BF16), mem2.shape[0] // bsz, BF16)
    vmem_ = _proj(mem2, xa_wv[i].astype(BF16), mem2.shape[0] // bsz, BF16)
    h = _attn(h, kmem, vmem_, xa_wq[i].astype(BF16), xa_wo[i].astype(BF16), _row(ln_mem_g[i]), _row(ln_mem_b[i]),
              bsz, 1024)
    return _ffn(h, ffn_w1[i].astype(BF16), ffn_w2[i].astype(BF16), _row(ln_ffn_g[i]), _row(ln_ffn_b[i]), 512)


def kernel(x, mem, ev_w_in, ev_mu, ev_w0, ev_w2, ev_a0, ev_a2, ev_g2, ev_k_k, ev_k_a, ev_r_k, ev_gn_g, ev_gn_b, ev_lru_conv_w, ev_lru_conv_b, ev_lru_wa, ev_lru_ba, ev_lru_wx, ev_lru_bx, ev_lru_lam, ev_w_out, od_w_in, od_cf_conv_w, od_cf_conv_b, od_cf_ln_g, od_cf_ln_b, od_ssm_conv_w, od_ssm_conv_b, od_dt_bias, od_a_log, od_d_skip, od_ssm_norm_g, od_w_out, xa_wq, xa_wk, xa_wv, xa_wo, ffn_w1, ffn_w2, ln_mix_g, ln_mix_b, ln_mem_g, ln_mem_b, ln_ffn_g, ln_ffn_b):
    bsz, seq, _ = x.shape
    n = bsz * seq
    h = x.reshape(n, D_MODEL)
    mem2 = mem.reshape(bsz * mem.shape[1], D_MODEL)
    xa = (xa_wq, xa_wk, xa_wv, xa_wo, ffn_w1, ffn_w2, ln_mem_g, ln_mem_b, ln_ffn_g, ln_ffn_b)

    rw = 3 * RWKV_WIDTH
    lo = 2 * LANES
    w_in = jnp.concatenate([ev_w_in[0][:, 0:rw], ev_w_in[0][:, rw + lo:], ev_w_in[0][:, rw:rw + lo]], axis=1)
    p = _proj(h, w_in.astype(BF16), 512)
    mu = ev_mu[0]
    rwkv_prm = (_row(mu[0:rw]), _row(mu[rw:rw + LANES]), _row(mu[rw + LANES:rw + lo]),
                _row(ev_w0[0]), _pad_rows(ev_w2[0], LANES, 0), _row(ev_a0[0]), _pad_rows(ev_a2[0], LANES, 64),
                ev_g2[0].astype(F32), _row(ev_k_k[0]), _row(ev_k_a[0]), _row(ev_r_k[0]), _row(ev_gn_g[0]),
                _row(ev_gn_b[0]), _head_sum_matrix(), _tril_ones(RWKV_CHUNK, RWKV_STEP_CHUNKS))
    lru_prm = (ev_lru_conv_w[0], _row(ev_lru_conv_b[0]), _block_diag(ev_lru_wa[0]).astype(BF16), _row(ev_lru_ba[0]),
               _block_diag(ev_lru_wx[0]).astype(BF16), _row(ev_lru_bx[0]), _row(ev_lru_lam[0]))
    y_a, y_b = _mix0(p, rwkv_prm, lru_prm, bsz, seq)
    w_out = ev_w_out[0].astype(BF16)
    h = _mix_out(y_a, y_b, w_out[0:RWKV_WIDTH], w_out[RWKV_WIDTH:], h, _row(ln_mix_g[0]), _row(ln_mix_b[0]), 512)
    h = _sublayers(h, mem2, 0, bsz, *xa)

    w_in = jnp.pad(od_w_in[0], ((0, 0), (0, ODD_IN_PAD - ODD_IN))).astype(BF16)
    p = _proj(h, w_in, 512)
    conf_prm = (od_cf_conv_w[0], _row(od_cf_conv_b[0]), _row(od_cf_ln_g[0]), _row(od_cf_ln_b[0]))
    y_c = _conf(p, conf_prm, bsz, seq)
    cw, cb = od_ssm_conv_w[0], od_ssm_conv_b[0]
    pad16 = lambda v: jnp.pad(v, (0, LANES - SSM_HEADS)).reshape(1, LANES).astype(F32)
    ssd_prm = (cw[:, 0:SSM_INNER], _row(cb[0:SSM_INNER]), cw[:, SSM_INNER:], _row(cb[SSM_INNER:]),
               pad16(od_dt_bias[0]), pad16(od_a_log[0]), _row(jnp.repeat(od_d_skip[0], SSM_HEAD_DIM)),
               _row(od_ssm_norm_g[0]), _head_expand_matrix(), _tril_ones(SSM_CHUNK, SSM_STEP_CHUNKS))
    y_d = _ssd(p, ssd_prm, bsz, seq)
    w_out = od_w_out[0].astype(BF16)
    h = _mix_out(y_c, y_d, w_out[0:CONF_WIDTH], w_out[CONF_WIDTH:], h, _row(ln_mix_g[1]), _row(ln_mix_b[1]), 512)
    h = _sublayers(h, mem2, 1, bsz, *xa)
    return h.reshape(bsz, seq, D_MODEL)
```

```python
import jax
import jax.numpy as jnp
import numpy as np
from jax import lax
from jax.experimental import pallas as pl
from jax.experimental.pallas import tpu as pltpu

F32 = jnp.float32
BF16 = jnp.bfloat16

D_MODEL = 1024
DEPTH = 2
DN_ALPHA = (2 * DEPTH) ** 0.25
LN_EPS = 1e-5

RWKV_WIDTH = 512
RWKV_HEAD_DIM = 64
RWKV_HEADS = 8
RWKV_GN_EPS = 64e-5
LOG2_E = 1.4426950408889634
HALF_DECAY = 0.6065306597126334
RWKV_CHUNK = 64
RWKV_STEP_CHUNKS = 4
RWKV_GROUP = 4

LRU_WIDTH = 512
LRU_C = 8.0
LRU_CONV = 4
LRU_TILE = 256

CONF_WIDTH = 512
CONF_CONV = 31
CONF_HALO = 32
CONF_TILE = 512

SSM_INNER = 1024
SSM_HEAD_DIM = 64
SSM_HEADS = 16
SSM_GROUPS = 2
SSM_STATE = 128
SSM_CONV = 4
SSM_CHUNK = 128
SSM_STEP_CHUNKS = 2
ODD_IN = 3600
ODD_IN_PAD = 3712

EPILOGUE_SPLIT = 2
XA_HEADS = 4
XA_HEAD_DIM = 256
D_FF = 4096

SUBLANES = 8
LANES = 128
VMEM_CAP = 56 * 1024 * 1024


def _cparams(semantics, vmem_bytes):
    return pltpu.CompilerParams(dimension_semantics=semantics,
                                vmem_limit_bytes=int(min(max(vmem_bytes, 16 * 1024 * 1024), VMEM_CAP)))


def _dot(a, b):
    return jnp.dot(a.astype(BF16), b.astype(BF16), preferred_element_type=F32)


def _split_parts(x, parts):
    out = []
    for _ in range(parts - 1):
        hi = x.astype(BF16)
        out.append(hi)
        x = x - hi.astype(F32)
    out.append(x.astype(BF16))
    return out


def _dot_onehot_rhs(x, m, parts):
    acc = None
    for xp in _split_parts(x, parts):
        t = jnp.dot(xp, m, preferred_element_type=F32)
        acc = t if acc is None else acc + t
    return acc


def _dot_onehot_lhs(m, x, parts):
    acc = None
    for xp in _split_parts(x, parts):
        t = jnp.dot(m, xp, preferred_element_type=F32)
        acc = t if acc is None else acc + t
    return acc


def _head_sums(terms, bdm):
    pieces, owner = [], []
    for n, (x, parts) in enumerate(terms):
        for p in _split_parts(x, parts):
            pieces.append(p)
            owner.append(n)
    rows = terms[0][0].shape[0]
    w = bdm.shape[0]
    stacked = jnp.concatenate(pieces, axis=0)
    sums = jnp.concatenate([jnp.dot(stacked[:, q * w:(q + 1) * w], bdm, preferred_element_type=F32)
                            for q in range(stacked.shape[1] // w)], axis=1)
    out = [None] * len(terms)
    for j, n in enumerate(owner):
        piece = sums[j * rows:(j + 1) * rows]
        out[n] = piece if out[n] is None else out[n] + piece
    return out


def _dot_nt(a, b):
    return lax.dot_general(a.astype(BF16), b.astype(BF16), (((1,), (1,)), ((), ())),
                           preferred_element_type=F32)


def _sigmoid(x):
    return 1.0 / (1.0 + jnp.exp(-x))


def _softplus(x):
    return jnp.maximum(x, 0.0) + jnp.log(1.0 + jnp.exp(-jnp.abs(x)))


def _silu(x):
    return x * _sigmoid(x)


def _gelu_tanh(x):
    return 0.5 * x * (1.0 + jnp.tanh(0.7978845608028654 * (x + 0.044715 * (x * x * x))))


def _res_ln(h, y, g, b):
    z = DN_ALPHA * h + y
    mu = jnp.mean(z, axis=-1, keepdims=True)
    zc = z - mu
    var = jnp.mean(zc * zc, axis=-1, keepdims=True)
    return zc * lax.rsqrt(var + LN_EPS) * g + b


def _proj_kernel(x_ref, w_ref, o_ref):
    o_ref[...] = _dot(x_ref[...], w_ref[...]).astype(o_ref.dtype)


def _proj(x, w, tm, out_dtype=F32):
    n, k = x.shape
    m = w.shape[1]
    vmem = 2 * (tm * k * x.dtype.itemsize + tm * m * jnp.dtype(out_dtype).itemsize) + k * m * 2 + (4 << 20)
    return pl.pallas_call(
        _proj_kernel,
        grid=(n // tm,),
        in_specs=[pl.BlockSpec((tm, k), lambda i: (i, 0)),
                  pl.BlockSpec((k, m), lambda i: (0, 0), pipeline_mode=pl.Buffered(1))],
        out_specs=pl.BlockSpec((tm, m), lambda i: (i, 0)),
        out_shape=jax.ShapeDtypeStruct((n, m), out_dtype),
        compiler_params=_cparams(("parallel",), vmem),
    )(x, w)


def _mix_out_kernel(a_ref, b_ref, wa_ref, wb_ref, h_ref, g_ref, bias_ref, o_ref):
    rows = o_ref.shape[0] // EPILOGUE_SPLIT
    ys = []
    for s in range(EPILOGUE_SPLIT):
        sl = slice(s * rows, (s + 1) * rows)
        ys.append(_dot(a_ref[sl, :], wa_ref[...]) + _dot(b_ref[sl, :], wb_ref[...]))
    for s in range(EPILOGUE_SPLIT):
        sl = slice(s * rows, (s + 1) * rows)
        o_ref[sl, :] = _res_ln(h_ref[sl, :], ys[s], g_ref[...], bias_ref[...])


def _mix_out(a, b, wa, wb, h, g, bias, tm):
    n = h.shape[0]
    ka, kb = a.shape[1], b.shape[1]
    vmem = 2 * (tm * (ka + kb) * 2 + 2 * tm * D_MODEL * 4 + (ka + kb) * D_MODEL * 2) + (8 << 20)
    row = lambda i: (i, 0)
    fix = lambda i: (0, 0)
    return pl.pallas_call(
        _mix_out_kernel,
        grid=(n // tm,),
        in_specs=[pl.BlockSpec((tm, ka), row), pl.BlockSpec((tm, kb), row),
                  pl.BlockSpec((ka, D_MODEL), fix), pl.BlockSpec((kb, D_MODEL), fix),
                  pl.BlockSpec((tm, D_MODEL), row), pl.BlockSpec((1, D_MODEL), fix),
                  pl.BlockSpec((1, D_MODEL), fix)],
        out_specs=pl.BlockSpec((tm, D_MODEL), row),
        out_shape=jax.ShapeDtypeStruct((n, D_MODEL), F32),
        compiler_params=_cparams(("parallel",), vmem),
    )(a, b, wa, wb, h, g, bias)


def _attn_kernel(h_ref, k_ref, v_ref, wq_ref, wo_ref, g_ref, b_ref, o_ref, cat_ref):
    rows = o_ref.shape[0] // EPILOGUE_SPLIT
    subs = [slice(s * rows, (s + 1) * rows) for s in range(EPILOGUE_SPLIT)]
    q = [_dot(h_ref[rs, :], wq_ref[...]) for rs in subs]
    for hd in range(XA_HEADS):
        sl = slice(hd * XA_HEAD_DIM, (hd + 1) * XA_HEAD_DIM)
        for n, rs in enumerate(subs):
            s = _dot_nt(q[n][:, sl], k_ref[:, sl]) * (XA_HEAD_DIM ** -0.5)
            s = s - jnp.max(s, axis=-1, keepdims=True)
            e = jnp.exp(s)
            p = e / jnp.sum(e, axis=-1, keepdims=True)
            cat_ref[rs, sl] = _dot(p, v_ref[:, sl])
    y = [_dot(cat_ref[rs, :], wo_ref[...]) for rs in subs]
    for n, rs in enumerate(subs):
        o_ref[rs, :] = _res_ln(h_ref[rs, :], y[n], g_ref[...], b_ref[...])


def _attn(h, kmem, vmem_, wq, wo, g, b, bsz, tm):
    n = h.shape[0]
    tiles = n // bsz // tm
    n_mem = kmem.shape[0] // bsz
    vmem = 2 * (2 * tm * D_MODEL * 4 + 2 * n_mem * D_MODEL * 2 + 2 * D_MODEL * D_MODEL * 2) + tm * D_MODEL * 16 + (8 << 20)
    row = lambda bi, i: (bi * tiles + i, 0)
    mem = lambda bi, i: (bi, 0)
    fix = lambda bi, i: (0, 0)
    return pl.pallas_call(
        _attn_kernel,
        grid=(bsz, tiles),
        in_specs=[pl.BlockSpec((tm, D_MODEL), row), pl.BlockSpec((n_mem, D_MODEL), mem),
                  pl.BlockSpec((n_mem, D_MODEL), mem), pl.BlockSpec((D_MODEL, D_MODEL), fix),
                  pl.BlockSpec((D_MODEL, D_MODEL), fix), pl.BlockSpec((1, D_MODEL), fix),
                  pl.BlockSpec((1, D_MODEL), fix)],
        out_specs=pl.BlockSpec((tm, D_MODEL), row),
        out_shape=jax.ShapeDtypeStruct((n, D_MODEL), F32),
        scratch_shapes=[pltpu.VMEM((tm, D_MODEL), F32)],
        compiler_params=_cparams(("parallel", "parallel"), vmem),
    )(h, kmem, vmem_, wq, wo, g, b)


def _ffn_kernel(h_ref, w1_ref, w2_ref, g_ref, b_ref, o_ref):
    rows = o_ref.shape[0] // EPILOGUE_SPLIT
    subs = [slice(s * rows, (s + 1) * rows) for s in range(EPILOGUE_SPLIT)]
    u = [jnp.square(jnp.maximum(_dot(h_ref[rs, :], w1_ref[...]), 0.0)).astype(BF16) for rs in subs]
    y = [_dot(u[n], w2_ref[...]) for n in range(EPILOGUE_SPLIT)]
    for n, rs in enumerate(subs):
        o_ref[rs, :] = _res_ln(h_ref[rs, :], y[n], g_ref[...], b_ref[...])


def _ffn(h, w1, w2, g, b, tm):
    n = h.shape[0]
    vmem = 2 * D_MODEL * D_FF * 2 + 4 * tm * D_MODEL * 4 + tm * D_FF * 6 + tm * D_MODEL * 6 + (6 << 20)
    fix = lambda i: (0, 0)
    return pl.pallas_call(
        _ffn_kernel,
        grid=(n // tm,),
        in_specs=[pl.BlockSpec((tm, D_MODEL), lambda i: (i, 0)),
                  pl.BlockSpec((D_MODEL, D_FF), fix, pipeline_mode=pl.Buffered(1)),
                  pl.BlockSpec((D_FF, D_MODEL), fix, pipeline_mode=pl.Buffered(1)),
                  pl.BlockSpec((1, D_MODEL), fix),
                  pl.BlockSpec((1, D_MODEL), fix)],
        out_specs=pl.BlockSpec((tm, D_MODEL), lambda i: (i, 0)),
        out_shape=jax.ShapeDtypeStruct((n, D_MODEL), F32),
        compiler_params=_cparams(("parallel",), vmem),
    )(h, w1, w2, g, b)


def _shifted(ext_ref, x, first, rows):
    if first is not None:
        @pl.when(first)
        def _():
            ext_ref[...] = jnp.zeros(ext_ref.shape, F32)

    ext = jnp.concatenate([ext_ref[...], x], axis=0)
    prev = pltpu.roll(ext, 1, axis=0)[SUBLANES:SUBLANES + rows, :]
    ext_ref[...] = x[rows - SUBLANES:rows, :]
    return prev


def _causal_conv(ext_ref, x, w_ref, b_ref, first, rows, taps, halo):
    if first is not None:
        @pl.when(first)
        def _():
            ext_ref[...] = jnp.zeros(ext_ref.shape, F32)

    ext = jnp.concatenate([ext_ref[...], x], axis=0)
    acc = x * w_ref[taps - 1:taps, :] + b_ref[...]
    for b in range(min(SUBLANES, taps)):
        shifted = ext if b == 0 else pltpu.roll(ext, b, axis=0)
        for a in range((taps - 1 - b) // SUBLANES + 1):
            lag = SUBLANES * a + b
            if lag > 0:
                start = halo - SUBLANES * a
                acc = acc + shifted[start:start + rows, :] * w_ref[taps - 1 - lag:taps - lag, :]
    ext_ref[...] = x[rows - halo:rows, :]
    return acc


def _lru_stages(xb_ref, gb_ref, cw_ref, cb_ref, wa_ref, ba_ref, wx_ref, bx_ref, lam_ref, o_ref, ext_ref, carry_ref):
    rows = LRU_TILE
    xc = _causal_conv(ext_ref, xb_ref[...], cw_ref, cb_ref, None, rows, LRU_CONV, SUBLANES)
    yield
    gate_r = _sigmoid(_dot(xc, wa_ref[...]) + ba_ref[...])
    gate_i = _sigmoid(_dot(xc, wx_ref[...]) + bx_ref[...])
    yield
    log_a = -LRU_C * _softplus(-lam_ref[...]) * gate_r
    a = jnp.exp(log_a)
    u = jnp.sqrt(1.0 - a * a) * (gate_i * xc)
    yield
    ridx = lax.broadcasted_iota(jnp.int32, (rows, LRU_WIDTH), 0)
    s = 1
    while s < SUBLANES:
        keep = ridx >= s
        a_sh = jnp.where(keep, pltpu.roll(a, s, axis=0), 1.0)
        u_sh = jnp.where(keep, pltpu.roll(u, s, axis=0), 0.0)
        u = a * u_sh + u
        a = a * a_sh
        s *= 2
        yield
    while s < rows:
        u = jnp.concatenate([u[0:s], a[s:rows] * u[0:rows - s] + u[s:rows]], axis=0)
        a = jnp.concatenate([a[0:s], a[s:rows] * a[0:rows - s]], axis=0)
        s *= 2
        yield
    h = a * carry_ref[0:1, :] + u
    carry_ref[...] = jnp.broadcast_to(h[rows - 1:rows, :], carry_ref.shape)
    o_ref[...] = (h * _gelu_tanh(gb_ref[...])).astype(o_ref.dtype)


def _mix0_kernel(rkv_ref, wa_ref, gd_ref, mu_rkv_ref, mu_wa_ref, mu_gd_ref, w0_ref, w2_ref, a0_ref, a2_ref,
                 g2_ref, kk_ref, ka_ref, rk_ref, gng_ref, gnb_ref, bdm_ref, tril_ref,
                 xb_ref, gb_ref, cw_ref, cb_ref, lwa_ref, lba_ref, lwx_ref, lbx_ref, lam_ref,
                 o_ref, o_lru_ref,
                 ext_rkv, ext_wa, ext_gd, state_ref, obuf_ref, ext_lru, carry_lru):
    L = RWKV_CHUNK
    CH = RWKV_STEP_CHUNKS
    rows = L * CH
    N = RWKV_HEAD_DIM

    @pl.when(pl.program_id(1) == 0)
    def _():
        state_ref[...] = jnp.zeros(state_ref.shape, F32)
        carry_lru[...] = jnp.zeros(carry_lru.shape, F32)
        for ext in (ext_rkv, ext_wa, ext_gd, ext_lru):
            ext[...] = jnp.zeros(ext.shape, F32)

    lru = _lru_stages(xb_ref, gb_ref, cw_ref, cb_ref, lwa_ref, lba_ref, lwx_ref, lbx_ref, lam_ref, o_lru_ref,
                      ext_lru, carry_lru)
    tick = lambda: next(lru, None)

    x = rkv_ref[...]
    xm = x + (_shifted(ext_rkv, x, None, rows) - x) * mu_rkv_ref[...]
    wa = wa_ref[...]
    wam = wa + (_shifted(ext_wa, wa, None, rows) - wa) * mu_wa_ref[...]
    gd = gd_ref[...]
    gdm = gd + (_shifted(ext_gd, gd, None, rows) - gd) * mu_gd_ref[...]

    r = xm[:, 0:RWKV_WIDTH]
    k = xm[:, RWKV_WIDTH:2 * RWKV_WIDTH]
    v = xm[:, 2 * RWKV_WIDTH:3 * RWKV_WIDTH]
    z = w0_ref[...] + _dot(jnp.tanh(wam), w2_ref[...])
    logdec = (-LOG2_E * HALF_DECAY) * _sigmoid(z)
    a = _sigmoid(a0_ref[...] + _dot(wam, a2_ref[...]))
    g = _dot(_sigmoid(gdm), g2_ref[...])
    bdm = bdm_ref[...]
    kks = k * kk_ref[...]
    k2 = k * (1.0 + (a - 1.0) * ka_ref[...])
    kk_sq, bonus_s = _head_sums([(kks * kks, 2), (r * k2 * rk_ref[...], 1)], bdm)
    kk = kks * lax.rsqrt(jnp.maximum(kk_sq, 1e-24))
    bonus = bonus_s * v
    beta = kk * a

    cum = _dot_onehot_lhs(tril_ref[...], logdec, 2)
    cum_l = jnp.concatenate(
        [jnp.broadcast_to(cum[c * L + L - 1:c * L + L, :], (L, RWKV_WIDTH)) for c in range(CH)], axis=0)
    e_to_end = jnp.exp2(cum_l - cum)
    e_neg = jnp.exp2(-cum)
    a_t = -kk * jnp.exp2(cum - logdec)
    r_t = r * jnp.exp2(cum)
    b_t = beta * e_neg
    k_t = k2 * e_neg
    b_p = beta * e_to_end
    k_p = k2 * e_to_end
    p_l = jnp.exp2(cum_l)

    W = RWKV_GROUP * N
    bdm32 = bdm.astype(F32)
    lane_s = jnp.bitwise_and(lax.broadcasted_iota(jnp.int32, (L, W), 1), N - 1)
    t_idx = lax.broadcasted_iota(jnp.int32, (L, W), 0)
    strict = lane_s < t_idx
    incl = lane_s <= t_idx
    eye_cat = jnp.where(lane_s == t_idx, 1.0, 0.0)
    mask2 = jnp.concatenate([strict, incl], axis=0)

    per_half = LANES // N
    half_mask = [bdm[j * N:j * N + L, 0:LANES] for j in range(per_half)]
    zero_half = jnp.zeros((L, LANES), BF16)

    def bd(y):
        yb = y.astype(BF16)
        blocks = []
        for h in range(RWKV_GROUP):
            half, j = divmod(h, per_half)
            part = yb[:, half * LANES:(half + 1) * LANES] * half_mask[j]
            blocks.append(jnp.concatenate([part if c == half else zero_half for c in range(W // LANES)], axis=1))
        return jnp.concatenate(blocks, axis=0)

    mmb = lambda x, y: jnp.dot(x.astype(BF16), y, preferred_element_type=F32)
    mmb_nt = lambda x, y: lax.dot_general(x.astype(BF16), y, (((1,), (1,)), ((), ())), preferred_element_type=F32)

    items = [(c, q) for c in range(CH) for q in range(RWKV_WIDTH // W)]
    cut = lambda arr, c, q: arr[c * L:(c + 1) * L, q * W:(q + 1) * W]
    a_c = {i: cut(a_t, *i) for i in items}
    r_c = {i: cut(r_t, *i) for i in items}
    v_c = {i: cut(v, *i) for i in items}
    bp_c = {i: cut(b_p, *i) for i in items}
    lhs = {i: jnp.concatenate([a_c[i], r_c[i]], axis=0).astype(BF16) for i in items}
    q_b = {i: mmb_nt(lhs[i], bd(cut(b_t, *i))) for i in items}
    tick()
    q_k = {i: jnp.where(mask2, mmb_nt(lhs[i], bd(cut(k_t, *i))), 0.0) for i in items}
    tick()
    qkv = {i: mmb(q_k[i], bd(v_c[i])) for i in items}
    tick()
    pw = {i: jnp.where(strict, q_b[i][0:L], 0.0) for i in items}
    tinv = {i: eye_cat + pw[i] for i in items}
    pw = {i: mmb(pw[i], bd(pw[i])) for i in items}
    tick()
    for _ in range(L.bit_length() - 3):
        both = {i: mmb(jnp.concatenate([pw[i], tinv[i]], axis=0), bd(pw[i])) for i in items}
        pw = {i: both[i][0:L] for i in items}
        tinv = {i: tinv[i] + both[i][L:2 * L] for i in items}
        tick()
    tinv = {i: tinv[i] + mmb(tinv[i], bd(pw[i])) for i in items}
    tick()
    ta = {i: mmb(tinv[i], bd(a_c[i])) for i in items}
    w2 = {i: mmb(tinv[i], bd(qkv[i][0:L])) for i in items}
    tick()
    a_rb = {i: jnp.where(incl, q_b[i][L:2 * L], 0.0).astype(BF16) for i in items}
    q_t = {i: r_c[i] + mmb(a_rb[i], bd(ta[i])) for i in items}
    o_loc = {i: mmb(a_rb[i], bd(w2[i])) + qkv[i][L:2 * L] for i in items}
    tick()
    gk = {i: _dot(ta[i].T, bp_c[i]).astype(BF16) * bdm for i in items}
    s_loc = {i: _dot(jnp.concatenate([w2[i], v_c[i]], axis=0).T,
                     jnp.concatenate([bp_c[i], cut(k_p, *i)], axis=0)) * bdm32 for i in items}
    tick()
    state = [state_ref[q] for q in range(RWKV_WIDTH // W)]
    for c in range(CH):
        for q in range(RWKV_WIDTH // W):
            i = (c, q)
            obuf_ref[c * L:(c + 1) * L, q * W:(q + 1) * W] = mmb_nt(q_t[i], state[q].astype(BF16)) + o_loc[i]
            state[q] = state[q] * p_l[c * L:c * L + 1, q * W:(q + 1) * W] + mmb(state[q], gk[i]) + s_loc[i]
        tick()
    for q in range(RWKV_WIDTH // W):
        state_ref[q] = state[q]

    o = obuf_ref[...]
    oc = o - _head_sums([(o, 1)], bdm)[0] * (1.0 / N)
    var = _head_sums([(oc * oc, 1)], bdm)[0] * (1.0 / N)
    o_ref[...] = ((oc * lax.rsqrt(var + RWKV_GN_EPS) * gng_ref[...] + gnb_ref[...] + bonus) * g).astype(o_ref.dtype)
    for _ in lru:
        pass


def _mix0(p, rwkv_prm, lru_prm, bsz, seq):
    L = RWKV_CHUNK * RWKV_STEP_CHUNKS
    assert L == LRU_TILE
    tiles = seq // L
    n = bsz * seq
    gw = RWKV_GROUP * RWKV_HEAD_DIM
    row = lambda c: (lambda bi, i: (bi * tiles + i, c))
    fix = lambda bi, i: (0, 0)
    vec = lambda width: pl.BlockSpec((1, width), fix)
    mat = lambda r, c: pl.BlockSpec((r, c), fix)
    in_specs = [pl.BlockSpec((L, 3 * RWKV_WIDTH), row(0)),
                pl.BlockSpec((L, LANES), row(20)), pl.BlockSpec((L, LANES), row(21)),
                vec(3 * RWKV_WIDTH), vec(LANES), vec(LANES),
                vec(RWKV_WIDTH), mat(LANES, RWKV_WIDTH), vec(RWKV_WIDTH), mat(LANES, RWKV_WIDTH),
                mat(LANES, RWKV_WIDTH),
                vec(RWKV_WIDTH), vec(RWKV_WIDTH), vec(RWKV_WIDTH), vec(RWKV_WIDTH), vec(RWKV_WIDTH),
                mat(gw, gw), mat(L, L),
                pl.BlockSpec((L, LRU_WIDTH), row(3)), pl.BlockSpec((L, LRU_WIDTH), row(4)),
                mat(LRU_CONV, LRU_WIDTH), vec(LRU_WIDTH), mat(LRU_WIDTH, LRU_WIDTH), vec(LRU_WIDTH),
                mat(LRU_WIDTH, LRU_WIDTH), vec(LRU_WIDTH), vec(LRU_WIDTH)]
    out_row = lambda bi, i: (bi * tiles + i, 0)
    return pl.pallas_call(
        _mix0_kernel,
        grid=(bsz, tiles),
        in_specs=in_specs,
        out_specs=[pl.BlockSpec((L, RWKV_WIDTH), out_row), pl.BlockSpec((L, LRU_WIDTH), out_row)],
        out_shape=[jax.ShapeDtypeStruct((n, RWKV_WIDTH), BF16), jax.ShapeDtypeStruct((n, LRU_WIDTH), BF16)],
        scratch_shapes=[pltpu.VMEM((SUBLANES, 3 * RWKV_WIDTH), F32),
                        pltpu.VMEM((SUBLANES, LANES), F32),
                        pltpu.VMEM((SUBLANES, LANES), F32),
                        pltpu.VMEM((RWKV_HEADS // RWKV_GROUP, gw, gw), F32),
                        pltpu.VMEM((L, RWKV_WIDTH), F32),
                        pltpu.VMEM((SUBLANES, LRU_WIDTH), F32),
                        pltpu.VMEM((SUBLANES, LRU_WIDTH), F32)],
        compiler_params=_cparams(("arbitrary", "arbitrary"), 48 << 20),
    )(p, p, p, *rwkv_prm, p, p, *lru_prm)


def _conf_kernel(c_ref, cw_ref, cb_ref, g_ref, b_ref, o_ref, ext_ref):
    rows = CONF_TILE
    first = pl.program_id(1) == 0
    c = c_ref[...]
    glu = c[:, 0:CONF_WIDTH] * _sigmoid(c[:, CONF_WIDTH:2 * CONF_WIDTH])
    u = _causal_conv(ext_ref, glu, cw_ref, cb_ref, first, rows, CONF_CONV, CONF_HALO)
    mu = jnp.mean(u, axis=-1, keepdims=True)
    uc = u - mu
    var = jnp.mean(uc * uc, axis=-1, keepdims=True)
    o_ref[...] = _silu(uc * lax.rsqrt(var + LN_EPS) * g_ref[...] + b_ref[...]).astype(o_ref.dtype)


def _conf(p, prm, bsz, seq):
    rows = CONF_TILE
    tiles = seq // rows
    n = bsz * seq
    fix = lambda bi, i: (0, 0)
    vec = pl.BlockSpec((1, CONF_WIDTH), fix)
    return pl.pallas_call(
        _conf_kernel,
        grid=(bsz, tiles),
        in_specs=[pl.BlockSpec((rows, 2 * CONF_WIDTH), lambda bi, i: (bi * tiles + i, 0)),
                  pl.BlockSpec((CONF_CONV, CONF_WIDTH), fix), vec, vec, vec],
        out_specs=pl.BlockSpec((rows, CONF_WIDTH), lambda bi, i: (bi * tiles + i, 0)),
        out_shape=jax.ShapeDtypeStruct((n, CONF_WIDTH), BF16),
        scratch_shapes=[pltpu.VMEM((CONF_HALO, CONF_WIDTH), F32)],
        compiler_params=_cparams(("arbitrary", "arbitrary"), 32 << 20),
    )(p, *prm)


def _ssd_kernel(z_ref, xs_ref, bc_ref, dt_ref, cwx_ref, cbx_ref, cwb_ref, cbb_ref, dtb_ref, alog_ref,
                dskip_ref, normg_ref, expand_ref, tril_ref, o_ref, ext_xs, ext_bc, state_ref, ybuf_ref):
    L = SSM_CHUNK
    CH = SSM_STEP_CHUNKS
    rows = L * CH
    P = SSM_HEAD_DIM
    NS = SSM_STATE
    HG = SSM_HEADS // SSM_GROUPS
    first = pl.program_id(1) == 0

    @pl.when(first)
    def _():
        state_ref[...] = jnp.zeros(state_ref.shape, F32)

    xs = _silu(_causal_conv(ext_xs, xs_ref[...], cwx_ref, cbx_ref, first, rows, SSM_CONV, SUBLANES))
    bc = _silu(_causal_conv(ext_bc, bc_ref[...], cwb_ref, cbb_ref, first, rows, SSM_CONV, SUBLANES))
    dt = _softplus(dt_ref[...] + dtb_ref[...])
    a_head = -jnp.exp(alog_ref[...])
    acum = _dot_onehot_lhs(tril_ref[...], dt * (a_head * LOG2_E), 3)
    acum_x = _dot_onehot_rhs(acum, expand_ref[...], 3)
    acum_xl = jnp.concatenate(
        [jnp.broadcast_to(acum_x[c * L + L - 1:c * L + L, :], (L, SSM_INNER)) for c in range(CH)], axis=0)
    xdt = xs * _dot_onehot_rhs(dt, expand_ref[...], 3)
    x_end = (xdt * jnp.exp2(acum_xl - acum_x)).astype(BF16)
    from_start = jnp.exp2(acum_x)
    chunk_decay = jnp.exp2(acum_xl)
    causal = lax.broadcasted_iota(jnp.int32, (L, L), 1) <= lax.broadcasted_iota(jnp.int32, (L, L), 0)

    half = SSM_INNER // SSM_GROUPS
    state = state_ref[...]
    cb = {}
    for c in range(CH):
        cs = slice(c * L, (c + 1) * L)
        new_state = []
        for g in range(SSM_GROUPS):
            gs = slice(g * half, (g + 1) * half)
            bm = bc[cs, g * NS:(g + 1) * NS]
            cm = bc[cs, (SSM_GROUPS + g) * NS:(SSM_GROUPS + g + 1) * NS].astype(BF16)
            cb[c, g] = _dot_nt(cm, bm)
            ybuf_ref[cs, gs] = _dot(cm, state[:, gs]) * from_start[cs, gs]
            new_state.append(state[:, gs] * chunk_decay[c * L:c * L + 1, gs] + _dot(bm.T, x_end[cs, gs]))
        state = jnp.concatenate(new_state, axis=1)
    state_ref[...] = state
    y_off = ybuf_ref[...]
    for c in range(CH):
        cs = slice(c * L, (c + 1) * L)
        acum_c = acum[cs, :]
        acum_t = acum_c.T
        for h in range(SSM_HEADS):
            decay = jnp.exp2(jnp.where(causal, acum_c[:, h:h + 1] - acum_t[h:h + 1, :], -jnp.inf))
            ybuf_ref[cs, h * P:(h + 1) * P] = _dot(cb[c, h // HG] * decay, xdt[cs, h * P:(h + 1) * P])

    y = (ybuf_ref[...] + y_off + dskip_ref[...] * xs) * _silu(z_ref[...])
    for g in range(SSM_GROUPS):
        yg = y[:, g * half:(g + 1) * half]
        ms = jnp.mean(yg * yg, axis=-1, keepdims=True)
        o_ref[:, g * half:(g + 1) * half] = (yg * lax.rsqrt(ms + LN_EPS)
                                             * normg_ref[:, g * half:(g + 1) * half]).astype(o_ref.dtype)


def _ssd(p, prm, bsz, seq):
    L = SSM_CHUNK * SSM_STEP_CHUNKS
    tiles = seq // L
    n = bsz * seq
    row = lambda c: (lambda bi, i: (bi * tiles + i, c))
    fix = lambda bi, i: (0, 0)
    bcw = 2 * SSM_GROUPS * SSM_STATE
    vec = lambda width: pl.BlockSpec((1, width), fix)
    in_specs = [pl.BlockSpec((L, SSM_INNER), row(1)), pl.BlockSpec((L, SSM_INNER), row(2)),
                pl.BlockSpec((L, bcw), row(6)), pl.BlockSpec((L, LANES), row(28)),
                pl.BlockSpec((SSM_CONV, SSM_INNER), fix), vec(SSM_INNER),
                pl.BlockSpec((SSM_CONV, bcw), fix), vec(bcw),
                vec(LANES), vec(LANES), vec(SSM_INNER), vec(SSM_INNER),
                pl.BlockSpec((LANES, SSM_INNER), fix), pl.BlockSpec((L, L), fix)]
    return pl.pallas_call(
        _ssd_kernel,
        grid=(bsz, tiles),
        in_specs=in_specs,
        out_specs=pl.BlockSpec((L, SSM_INNER), lambda bi, i: (bi * tiles + i, 0)),
        out_shape=jax.ShapeDtypeStruct((n, SSM_INNER), BF16),
        scratch_shapes=[pltpu.VMEM((SUBLANES, SSM_INNER), F32),
                        pltpu.VMEM((SUBLANES, bcw), F32),
                        pltpu.VMEM((SSM_STATE, SSM_INNER), F32),
                        pltpu.VMEM((L, SSM_INNER), F32)],
        compiler_params=_cparams(("arbitrary", "arbitrary"), 40 << 20),
    )(p, p, p, p, *prm)


def _row(v):
    return v.reshape(1, -1).astype(F32)


def _pad_rows(m, rows, offset):
    out = jnp.zeros((rows, m.shape[1]), F32)
    return out.at[offset:offset + m.shape[0]].set(m)


def _block_diag(w):
    nb, d, e = w.shape
    eye = jnp.eye(nb, dtype=w.dtype)
    return (eye[:, None, :, None] * w[:, :, None, :]).reshape(nb * d, nb * e)


def _tril_ones(n, blocks=1):
    return jnp.asarray(np.kron(np.eye(blocks, dtype=np.float32), np.tril(np.ones((n, n), np.float32))), dtype=BF16)


def _head_sum_matrix():
    idx = np.arange(RWKV_GROUP * RWKV_HEAD_DIM) // RWKV_HEAD_DIM
    return jnp.asarray((idx[:, None] == idx[None, :]).astype(np.float32), dtype=BF16)


def _head_expand_matrix():
    m = np.zeros((LANES, SSM_INNER), np.float32)
    for h in range(SSM_HEADS):
        m[h, h * SSM_HEAD_DIM:(h + 1) * SSM_HEAD_DIM] = 1.0
    return jnp.asarray(m, dtype=BF16)


def _sublayers(h, mem2, i, bsz, xa_wq, xa_wk, xa_wv, xa_wo, ffn_w1, ffn_w2, ln_mem_g, ln_mem_b, ln_ffn_g, ln_ffn_b):
    kmem = _proj(mem2, xa_wk[i].astype(BF16), mem2.shape[0] // bsz, BF16)
    vmem_ = _proj(mem2, xa_wv[i].astype(BF16), mem2.shape[0] // bsz, BF16)
    h = _attn(h, kmem, vmem_, xa_wq[i].astype(BF16), xa_wo[i].astype(BF16), _row(ln_mem_g[i]), _row(ln_mem_b[i]),
              bsz, 1024)
    return _ffn(h, ffn_w1[i].astype(BF16), ffn_w2[i].astype(BF16), _row(ln_ffn_g[i]), _row(ln_ffn_b[i]), 512)


def kernel(x, mem, ev_w_in, ev_mu, ev_w0, ev_w2, ev_a0, ev_a2, ev_g2, ev_k_k, ev_k_a, ev_r_k, ev_gn_g, ev_gn_b, ev_lru_conv_w, ev_lru_conv_b, ev_lru_wa, ev_lru_ba, ev_lru_wx, ev_lru_bx, ev_lru_lam, ev_w_out, od_w_in, od_cf_conv_w, od_cf_conv_b, od_cf_ln_g, od_cf_ln_b, od_ssm_conv_w, od_ssm_conv_b, od_dt_bias, od_a_log, od_d_skip, od_ssm_norm_g, od_w_out, xa_wq, xa_wk, xa_wv, xa_wo, ffn_w1, ffn_w2, ln_mix_g, ln_mix_b, ln_mem_g, ln_mem_b, ln_ffn_g, ln_ffn_b):
    bsz, seq, _ = x.shape
    n = bsz * seq
    h = x.reshape(n, D_MODEL)
    mem2 = mem.reshape(bsz * mem.shape[1], D_MODEL)
    xa = (xa_wq, xa_wk, xa_wv, xa_wo, ffn_w1, ffn_w2, ln_mem_g, ln_mem_b, ln_ffn_g, ln_ffn_b)

    rw = 3 * RWKV_WIDTH
    lo = 2 * LANES
    w_in = jnp.concatenate([ev_w_in[0][:, 0:rw], ev_w_in[0][:, rw + lo:], ev_w_in[0][:, rw:rw + lo]], axis=1)
    p = _proj(h, w_in.astype(BF16), 512)
    mu = ev_mu[0]
    rwkv_prm = (_row(mu[0:rw]), _row(mu[rw:rw + LANES]), _row(mu[rw + LANES:rw + lo]),
                _row(ev_w0[0]), _pad_rows(ev_w2[0], LANES, 0), _row(ev_a0[0]), _pad_rows(ev_a2[0], LANES, 64),
                ev_g2[0].astype(F32), _row(ev_k_k[0]), _row(ev_k_a[0]), _row(ev_r_k[0]), _row(ev_gn_g[0]),
                _row(ev_gn_b[0]), _head_sum_matrix(), _tril_ones(RWKV_CHUNK, RWKV_STEP_CHUNKS))
    lru_prm = (ev_lru_conv_w[0], _row(ev_lru_conv_b[0]), _block_diag(ev_lru_wa[0]).astype(BF16), _row(ev_lru_ba[0]),
               _block_diag(ev_lru_wx[0]).astype(BF16), _row(ev_lru_bx[0]), _row(ev_lru_lam[0]))
    y_a, y_b = _mix0(p, rwkv_prm, lru_prm, bsz, seq)
    w_out = ev_w_out[0].astype(BF16)
    h = _mix_out(y_a, y_b, w_out[0:RWKV_WIDTH], w_out[RWKV_WIDTH:], h, _row(ln_mix_g[0]), _row(ln_mix_b[0]), 1024)
    h = _sublayers(h, mem2, 0, bsz, *xa)

    w_in = jnp.pad(od_w_in[0], ((0, 0), (0, ODD_IN_PAD - ODD_IN))).astype(BF16)
    p = _proj(h, w_in, 512)
    conf_prm = (od_cf_conv_w[0], _row(od_cf_conv_b[0]), _row(od_cf_ln_g[0]), _row(od_cf_ln_b[0]))
    y_c = _conf(p, conf_prm, bsz, seq)
    cw, cb = od_ssm_conv_w[0], od_ssm_conv_b[0]
    pad16 = lambda v: jnp.pad(v, (0, LANES - SSM_HEADS)).reshape(1, LANES).astype(F32)
    ssd_prm = (cw[:, 0:SSM_INNER], _row(cb[0:SSM_INNER]), cw[:, SSM_INNER:], _row(cb[SSM_INNER:]),
               pad16(od_dt_bias[0]), pad16(od_a_log[0]), _row(jnp.repeat(od_d_skip[0], SSM_HEAD_DIM)),
               _row(od_ssm_norm_g[0]), _head_expand_matrix(), _tril_ones(SSM_CHUNK, SSM_STEP_CHUNKS))
    y_d = _ssd(p, ssd_prm, bsz, seq)
    w_out = od_w_out[0].astype(BF16)
    h = _mix_out(y_c, y_d, w_out[0:CONF_WIDTH], w_out[CONF_WIDTH:], h, _row(ln_mix_g[1]), _row(ln_mix_b[1]), 1024)
    h = _sublayers(h, mem2, 1, bsz, *xa)
    return h.reshape(bsz, seq, D_MODEL)
```

```python
import jax
import jax.numpy as jnp
import numpy as np
from jax import lax
from jax.experimental import pallas as pl
from jax.experimental.pallas import tpu as pltpu

F32 = jnp.float32
BF16 = jnp.bfloat16

D_MODEL = 1024
DEPTH = 2
DN_ALPHA = (2 * DEPTH) ** 0.25
LN_EPS = 1e-5

RWKV_WIDTH = 512
RWKV_HEAD_DIM = 64
RWKV_HEADS = 8
RWKV_GN_EPS = 64e-5
LOG2_E = 1.4426950408889634
HALF_DECAY = 0.6065306597126334
RWKV_CHUNK = 64
RWKV_STEP_CHUNKS = 4
RWKV_GROUP = 4

LRU_WIDTH = 512
LRU_C = 8.0
LRU_CONV = 4
LRU_TILE = 256

CONF_WIDTH = 512
CONF_CONV = 31
CONF_HALO = 32
CONF_TILE = 512

SSM_INNER = 1024
SSM_HEAD_DIM = 64
SSM_HEADS = 16
SSM_GROUPS = 2
SSM_STATE = 128
SSM_CONV = 4
SSM_CHUNK = 128
SSM_STEP_CHUNKS = 2
ODD_IN = 3600
ODD_IN_PAD = 3712

EPILOGUE_SPLIT = 2
XA_HEADS = 4
XA_HEAD_DIM = 256
D_FF = 4096

SUBLANES = 8
LANES = 128
VMEM_CAP = 56 * 1024 * 1024


def _cparams(semantics, vmem_bytes):
    return pltpu.CompilerParams(dimension_semantics=semantics,
                                vmem_limit_bytes=int(min(max(vmem_bytes, 16 * 1024 * 1024), VMEM_CAP)))


def _dot(a, b):
    return jnp.dot(a.astype(BF16), b.astype(BF16), preferred_element_type=F32)


def _split_parts(x, parts):
    out = []
    for _ in range(parts - 1):
        hi = x.astype(BF16)
        out.append(hi)
        x = x - hi.astype(F32)
    out.append(x.astype(BF16))
    return out


def _dot_onehot_rhs(x, m, parts):
    acc = None
    for xp in _split_parts(x, parts):
        t = jnp.dot(xp, m, preferred_element_type=F32)
        acc = t if acc is None else acc + t
    return acc


def _dot_onehot_lhs(m, x, parts):
    acc = None
    for xp in _split_parts(x, parts):
        t = jnp.dot(m, xp, preferred_element_type=F32)
        acc = t if acc is None else acc + t
    return acc


def _head_sums(terms, bdm):
    pieces, owner = [], []
    for n, (x, parts) in enumerate(terms):
        for p in _split_parts(x, parts):
            pieces.append(p)
            owner.append(n)
    rows = terms[0][0].shape[0]
    w = bdm.shape[0]
    stacked = jnp.concatenate(pieces, axis=0)
    sums = jnp.concatenate([jnp.dot(stacked[:, q * w:(q + 1) * w], bdm, preferred_element_type=F32)
                            for q in range(stacked.shape[1] // w)], axis=1)
    out = [None] * len(terms)
    for j, n in enumerate(owner):
        piece = sums[j * rows:(j + 1) * rows]
        out[n] = piece if out[n] is None else out[n] + piece
    return out


def _dot_nt(a, b):
    return lax.dot_general(a.astype(BF16), b.astype(BF16), (((1,), (1,)), ((), ())),
                           preferred_element_type=F32)


def _sigmoid(x):
    return 1.0 / (1.0 + jnp.exp(-x))


def _softplus(x):
    return jnp.maximum(x, 0.0) + jnp.log(1.0 + jnp.exp(-jnp.abs(x)))


def _silu(x):
    return x * _sigmoid(x)


def _gelu_tanh(x):
    return 0.5 * x * (1.0 + jnp.tanh(0.7978845608028654 * (x + 0.044715 * (x * x * x))))


def _res_ln(h, y, g, b):
    z = DN_ALPHA * h + y
    mu = jnp.mean(z, axis=-1, keepdims=True)
    zc = z - mu
    var = jnp.mean(zc * zc, axis=-1, keepdims=True)
    return zc * lax.rsqrt(var + LN_EPS) * g + b


def _proj_kernel(x_ref, w_ref, o_ref):
    o_ref[...] = _dot(x_ref[...], w_ref[...]).astype(o_ref.dtype)


def _proj(x, w, tm, out_dtype=F32):
    n, k = x.shape
    m = w.shape[1]
    vmem = 2 * (tm * k * x.dtype.itemsize + tm * m * jnp.dtype(out_dtype).itemsize) + k * m * 2 + (4 << 20)
    return pl.pallas_call(
        _proj_kernel,
        grid=(n // tm,),
        in_specs=[pl.BlockSpec((tm, k), lambda i: (i, 0)),
                  pl.BlockSpec((k, m), lambda i: (0, 0), pipeline_mode=pl.Buffered(1))],
        out_specs=pl.BlockSpec((tm, m), lambda i: (i, 0)),
        out_shape=jax.ShapeDtypeStruct((n, m), out_dtype),
        compiler_params=_cparams(("parallel",), vmem),
    )(x, w)


def _mix_out_kernel(a_ref, b_ref, wa_ref, wb_ref, h_ref, g_ref, bias_ref, o_ref):
    rows = o_ref.shape[0] // EPILOGUE_SPLIT
    ys = []
    for s in range(EPILOGUE_SPLIT):
        sl = slice(s * rows, (s + 1) * rows)
        ys.append(_dot(a_ref[sl, :], wa_ref[...]) + _dot(b_ref[sl, :], wb_ref[...]))
    for s in range(EPILOGUE_SPLIT):
        sl = slice(s * rows, (s + 1) * rows)
        o_ref[sl, :] = _res_ln(h_ref[sl, :], ys[s], g_ref[...], bias_ref[...])


def _mix_out(a, b, wa, wb, h, g, bias, tm):
    n = h.shape[0]
    ka, kb = a.shape[1], b.shape[1]
    vmem = 2 * (tm * (ka + kb) * 2 + 2 * tm * D_MODEL * 4 + (ka + kb) * D_MODEL * 2) + (8 << 20)
    row = lambda i: (i, 0)
    fix = lambda i: (0, 0)
    return pl.pallas_call(
        _mix_out_kernel,
        grid=(n // tm,),
        in_specs=[pl.BlockSpec((tm, ka), row), pl.BlockSpec((tm, kb), row),
                  pl.BlockSpec((ka, D_MODEL), fix), pl.BlockSpec((kb, D_MODEL), fix),
                  pl.BlockSpec((tm, D_MODEL), row), pl.BlockSpec((1, D_MODEL), fix),
                  pl.BlockSpec((1, D_MODEL), fix)],
        out_specs=pl.BlockSpec((tm, D_MODEL), row),
        out_shape=jax.ShapeDtypeStruct((n, D_MODEL), F32),
        compiler_params=_cparams(("parallel",), vmem),
    )(a, b, wa, wb, h, g, bias)


def _attn_kernel(h_ref, k_ref, v_ref, wq_ref, wo_ref, g_ref, b_ref, o_ref, cat_ref):
    rows = o_ref.shape[0] // EPILOGUE_SPLIT
    subs = [slice(s * rows, (s + 1) * rows) for s in range(EPILOGUE_SPLIT)]
    q = [_dot(h_ref[rs, :], wq_ref[...]) for rs in subs]
    for hd in range(XA_HEADS):
        sl = slice(hd * XA_HEAD_DIM, (hd + 1) * XA_HEAD_DIM)
        for n, rs in enumerate(subs):
            s = _dot_nt(q[n][:, sl], k_ref[:, sl]) * (XA_HEAD_DIM ** -0.5)
            s = s - jnp.max(s, axis=-1, keepdims=True)
            e = jnp.exp(s)
            p = e / jnp.sum(e, axis=-1, keepdims=True)
            cat_ref[rs, sl] = _dot(p, v_ref[:, sl])
    y = [_dot(cat_ref[rs, :], wo_ref[...]) for rs in subs]
    for n, rs in enumerate(subs):
        o_ref[rs, :] = _res_ln(h_ref[rs, :], y[n], g_ref[...], b_ref[...])


def _attn(h, kmem, vmem_, wq, wo, g, b, bsz, tm):
    n = h.shape[0]
    tiles = n // bsz // tm
    n_mem = kmem.shape[0] // bsz
    vmem = 2 * (2 * tm * D_MODEL * 4 + 2 * n_mem * D_MODEL * 2 + 2 * D_MODEL * D_MODEL * 2) + tm * D_MODEL * 16 + (8 << 20)
    row = lambda bi, i: (bi * tiles + i, 0)
    mem = lambda bi, i: (bi, 0)
    fix = lambda bi, i: (0, 0)
    return pl.pallas_call(
        _attn_kernel,
        grid=(bsz, tiles),
        in_specs=[pl.BlockSpec((tm, D_MODEL), row), pl.BlockSpec((n_mem, D_MODEL), mem),
                  pl.BlockSpec((n_mem, D_MODEL), mem), pl.BlockSpec((D_MODEL, D_MODEL), fix),
                  pl.BlockSpec((D_MODEL, D_MODEL), fix), pl.BlockSpec((1, D_MODEL), fix),
                  pl.BlockSpec((1, D_MODEL), fix)],
        out_specs=pl.BlockSpec((tm, D_MODEL), row),
        out_shape=jax.ShapeDtypeStruct((n, D_MODEL), F32),
        scratch_shapes=[pltpu.VMEM((tm, D_MODEL), F32)],
        compiler_params=_cparams(("parallel", "parallel"), vmem),
    )(h, kmem, vmem_, wq, wo, g, b)


def _ffn_kernel(h_ref, w1_ref, w2_ref, g_ref, b_ref, o_ref):
    rows = o_ref.shape[0] // EPILOGUE_SPLIT
    subs = [slice(s * rows, (s + 1) * rows) for s in range(EPILOGUE_SPLIT)]
    u = [jnp.square(jnp.maximum(_dot(h_ref[rs, :], w1_ref[...]), 0.0)).astype(BF16) for rs in subs]
    y = [_dot(u[n], w2_ref[...]) for n in range(EPILOGUE_SPLIT)]
    for n, rs in enumerate(subs):
        o_ref[rs, :] = _res_ln(h_ref[rs, :], y[n], g_ref[...], b_ref[...])


def _ffn(h, w1, w2, g, b, tm):
    n = h.shape[0]
    vmem = 2 * D_MODEL * D_FF * 2 + 4 * tm * D_MODEL * 4 + tm * D_FF * 6 + tm * D_MODEL * 6 + (6 << 20)
    fix = lambda i: (0, 0)
    return pl.pallas_call(
        _ffn_kernel,
        grid=(n // tm,),
        in_specs=[pl.BlockSpec((tm, D_MODEL), lambda i: (i, 0)),
                  pl.BlockSpec((D_MODEL, D_FF), fix, pipeline_mode=pl.Buffered(1)),
                  pl.BlockSpec((D_FF, D_MODEL), fix, pipeline_mode=pl.Buffered(1)),
                  pl.BlockSpec((1, D_MODEL), fix),
                  pl.BlockSpec((1, D_MODEL), fix)],
        out_specs=pl.BlockSpec((tm, D_MODEL), lambda i: (i, 0)),
        out_shape=jax.ShapeDtypeStruct((n, D_MODEL), F32),
        compiler_params=_cparams(("parallel",), vmem),
    )(h, w1, w2, g, b)


def _shifted(ext_ref, x, first, rows):
    if first is not None:
        @pl.when(first)
        def _():
            ext_ref[...] = jnp.zeros(ext_ref.shape, F32)

    ext = jnp.concatenate([ext_ref[...], x], axis=0)
    prev = pltpu.roll(ext, 1, axis=0)[SUBLANES:SUBLANES + rows, :]
    ext_ref[...] = x[rows - SUBLANES:rows, :]
    return prev


def _causal_conv(ext_ref, x, w_ref, b_ref, first, rows, taps, halo):
    if first is not None:
        @pl.when(first)
        def _():
            ext_ref[...] = jnp.zeros(ext_ref.shape, F32)

    ext = jnp.concatenate([ext_ref[...], x], axis=0)
    acc = x * w_ref[taps - 1:taps, :] + b_ref[...]
    for b in range(min(SUBLANES, taps)):
        shifted = ext if b == 0 else pltpu.roll(ext, b, axis=0)
        for a in range((taps - 1 - b) // SUBLANES + 1):
            lag = SUBLANES * a + b
            if lag > 0:
                start = halo - SUBLANES * a
                acc = acc + shifted[start:start + rows, :] * w_ref[taps - 1 - lag:taps - lag, :]
    ext_ref[...] = x[rows - halo:rows, :]
    return acc


def _lru_stages(xb_ref, gb_ref, cw_ref, cb_ref, wa_ref, ba_ref, wx_ref, bx_ref, lam_ref, o_ref, ext_ref, carry_ref):
    rows = LRU_TILE
    xc = _causal_conv(ext_ref, xb_ref[...], cw_ref, cb_ref, None, rows, LRU_CONV, SUBLANES)
    yield
    gate_r = _sigmoid(_dot(xc, wa_ref[...]) + ba_ref[...])
    gate_i = _sigmoid(_dot(xc, wx_ref[...]) + bx_ref[...])
    yield
    log_a = -LRU_C * _softplus(-lam_ref[...]) * gate_r
    a = jnp.exp(log_a)
    u = jnp.sqrt(1.0 - a * a) * (gate_i * xc)
    yield
    ridx = lax.broadcasted_iota(jnp.int32, (rows, LRU_WIDTH), 0)
    s = 1
    while s < SUBLANES:
        keep = ridx >= s
        a_sh = jnp.where(keep, pltpu.roll(a, s, axis=0), 1.0)
        u_sh = jnp.where(keep, pltpu.roll(u, s, axis=0), 0.0)
        u = a * u_sh + u
        a = a * a_sh
        s *= 2
        yield
    while s < rows:
        u = jnp.concatenate([u[0:s], a[s:rows] * u[0:rows - s] + u[s:rows]], axis=0)
        a = jnp.concatenate([a[0:s], a[s:rows] * a[0:rows - s]], axis=0)
        s *= 2
        yield
    h = a * carry_ref[0:1, :] + u
    carry_ref[...] = jnp.broadcast_to(h[rows - 1:rows, :], carry_ref.shape)
    o_ref[...] = (h * _gelu_tanh(gb_ref[...])).astype(o_ref.dtype)


def _mix0_kernel(rkv_ref, wa_ref, gd_ref, mu_rkv_ref, mu_wa_ref, mu_gd_ref, w0_ref, w2_ref, a0_ref, a2_ref,
                 g2_ref, kk_ref, ka_ref, rk_ref, gng_ref, gnb_ref, bdm_ref, tril_ref,
                 xb_ref, gb_ref, cw_ref, cb_ref, lwa_ref, lba_ref, lwx_ref, lbx_ref, lam_ref,
                 o_ref, o_lru_ref,
                 ext_rkv, ext_wa, ext_gd, state_ref, obuf_ref, ext_lru, carry_lru):
    L = RWKV_CHUNK
    CH = RWKV_STEP_CHUNKS
    rows = L * CH
    N = RWKV_HEAD_DIM

    @pl.when(pl.program_id(1) == 0)
    def _():
        state_ref[...] = jnp.zeros(state_ref.shape, F32)
        carry_lru[...] = jnp.zeros(carry_lru.shape, F32)
        for ext in (ext_rkv, ext_wa, ext_gd, ext_lru):
            ext[...] = jnp.zeros(ext.shape, F32)

    lru = _lru_stages(xb_ref, gb_ref, cw_ref, cb_ref, lwa_ref, lba_ref, lwx_ref, lbx_ref, lam_ref, o_lru_ref,
                      ext_lru, carry_lru)
    tick = lambda: next(lru, None)

    x = rkv_ref[...]
    xm = x + (_shifted(ext_rkv, x, None, rows) - x) * mu_rkv_ref[...]
    wa = wa_ref[...]
    wam = wa + (_shifted(ext_wa, wa, None, rows) - wa) * mu_wa_ref[...]
    gd = gd_ref[...]
    gdm = gd + (_shifted(ext_gd, gd, None, rows) - gd) * mu_gd_ref[...]

    r = xm[:, 0:RWKV_WIDTH]
    k = xm[:, RWKV_WIDTH:2 * RWKV_WIDTH]
    v = xm[:, 2 * RWKV_WIDTH:3 * RWKV_WIDTH]
    z = w0_ref[...] + _dot(jnp.tanh(wam), w2_ref[...])
    logdec = (-LOG2_E * HALF_DECAY) * _sigmoid(z)
    a = _sigmoid(a0_ref[...] + _dot(wam, a2_ref[...]))
    g = _dot(_sigmoid(gdm), g2_ref[...])
    bdm = bdm_ref[...]
    kks = k * kk_ref[...]
    k2 = k * (1.0 + (a - 1.0) * ka_ref[...])
    kk_sq, bonus_s = _head_sums([(kks * kks, 2), (r * k2 * rk_ref[...], 1)], bdm)
    kk = kks * lax.rsqrt(jnp.maximum(kk_sq, 1e-24))
    bonus = bonus_s * v
    beta = kk * a

    cum = _dot_onehot_lhs(tril_ref[...], logdec, 2)
    cum_l = jnp.concatenate(
        [jnp.broadcast_to(cum[c * L + L - 1:c * L + L, :], (L, RWKV_WIDTH)) for c in range(CH)], axis=0)
    e_to_end = jnp.exp2(cum_l - cum)
    e_neg = jnp.exp2(-cum)
    a_t = -kk * jnp.exp2(cum - logdec)
    r_t = r * jnp.exp2(cum)
    b_t = beta * e_neg
    k_t = k2 * e_neg
    b_p = beta * e_to_end
    k_p = k2 * e_to_end
    p_l = jnp.exp2(cum_l)

    W = RWKV_GROUP * N
    bdm32 = bdm.astype(F32)
    lane_s = jnp.bitwise_and(lax.broadcasted_iota(jnp.int32, (L, W), 1), N - 1)
    t_idx = lax.broadcasted_iota(jnp.int32, (L, W), 0)
    strict = lane_s < t_idx
    incl = lane_s <= t_idx
    eye_cat = jnp.where(lane_s == t_idx, 1.0, 0.0)
    mask2 = jnp.concatenate([strict, incl], axis=0)

    per_half = LANES // N
    half_mask = [bdm[j * N:j * N + L, 0:LANES] for j in range(per_half)]
    zero_half = jnp.zeros((L, LANES), BF16)

    def bd(y):
        yb = y.astype(BF16)
        blocks = []
        for h in range(RWKV_GROUP):
            half, j = divmod(h, per_half)
            part = yb[:, half * LANES:(half + 1) * LANES] * half_mask[j]
            blocks.append(jnp.concatenate([part if c == half else zero_half for c in range(W // LANES)], axis=1))
        return jnp.concatenate(blocks, axis=0)

    mmb = lambda x, y: jnp.dot(x.astype(BF16), y, preferred_element_type=F32)
    mmb_nt = lambda x, y: lax.dot_general(x.astype(BF16), y, (((1,), (1,)), ((), ())), preferred_element_type=F32)

    items = [(c, q) for c in range(CH) for q in range(RWKV_WIDTH // W)]
    cut = lambda arr, c, q: arr[c * L:(c + 1) * L, q * W:(q + 1) * W]
    a_c = {i: cut(a_t, *i) for i in items}
    r_c = {i: cut(r_t, *i) for i in items}
    v_c = {i: cut(v, *i) for i in items}
    bp_c = {i: cut(b_p, *i) for i in items}
    lhs = {i: jnp.concatenate([a_c[i], r_c[i]], axis=0).astype(BF16) for i in items}
    q_b = {i: mmb_nt(lhs[i], bd(cut(b_t, *i))) for i in items}
    tick()
    q_k = {i: jnp.where(mask2, mmb_nt(lhs[i], bd(cut(k_t, *i))), 0.0) for i in items}
    tick()
    qkv = {i: mmb(q_k[i], bd(v_c[i])) for i in items}
    tick()
    pw = {i: jnp.where(strict, q_b[i][0:L], 0.0) for i in items}
    tinv = {i: eye_cat + pw[i] for i in items}
    pw = {i: mmb(pw[i], bd(pw[i])) for i in items}
    tick()
    for _ in range(L.bit_length() - 3):
        both = {i: mmb(jnp.concatenate([pw[i], tinv[i]], axis=0), bd(pw[i])) for i in items}
        pw = {i: both[i][0:L] for i in items}
        tinv = {i: tinv[i] + both[i][L:2 * L] for i in items}
        tick()
    tinv = {i: tinv[i] + mmb(tinv[i], bd(pw[i])) for i in items}
    tick()
    ta = {i: mmb(tinv[i], bd(a_c[i])) for i in items}
    w2 = {i: mmb(tinv[i], bd(qkv[i][0:L])) for i in items}
    tick()
    a_rb = {i: jnp.where(incl, q_b[i][L:2 * L], 0.0).astype(BF16) for i in items}
    q_t = {i: r_c[i] + mmb(a_rb[i], bd(ta[i])) for i in items}
    o_loc = {i: mmb(a_rb[i], bd(w2[i])) + qkv[i][L:2 * L] for i in items}
    tick()
    gk = {i: _dot(ta[i].T, bp_c[i]).astype(BF16) * bdm for i in items}
    s_loc = {i: _dot(jnp.concatenate([w2[i], v_c[i]], axis=0).T,
                     jnp.concatenate([bp_c[i], cut(k_p, *i)], axis=0)) * bdm32 for i in items}
    tick()
    state = [state_ref[q] for q in range(RWKV_WIDTH // W)]
    for c in range(CH):
        for q in range(RWKV_WIDTH // W):
            i = (c, q)
            obuf_ref[c * L:(c + 1) * L, q * W:(q + 1) * W] = mmb_nt(q_t[i], state[q].astype(BF16)) + o_loc[i]
            state[q] = state[q] * p_l[c * L:c * L + 1, q * W:(q + 1) * W] + mmb(state[q], gk[i]) + s_loc[i]
        tick()
    for q in range(RWKV_WIDTH // W):
        state_ref[q] = state[q]

    o = obuf_ref[...]
    oc = o - _head_sums([(o, 1)], bdm)[0] * (1.0 / N)
    var = _head_sums([(oc * oc, 1)], bdm)[0] * (1.0 / N)
    o_ref[...] = ((oc * lax.rsqrt(var + RWKV_GN_EPS) * gng_ref[...] + gnb_ref[...] + bonus) * g).astype(o_ref.dtype)
    for _ in lru:
        pass


def _mix0(p, rwkv_prm, lru_prm, bsz, seq):
    L = RWKV_CHUNK * RWKV_STEP_CHUNKS
    assert L == LRU_TILE
    tiles = seq // L
    n = bsz * seq
    gw = RWKV_GROUP * RWKV_HEAD_DIM
    row = lambda c: (lambda bi, i: (bi * tiles + i, c))
    fix = lambda bi, i: (0, 0)
    vec = lambda width: pl.BlockSpec((1, width), fix)
    mat = lambda r, c: pl.BlockSpec((r, c), fix)
    in_specs = [pl.BlockSpec((L, 3 * RWKV_WIDTH), row(0)),
                pl.BlockSpec((L, LANES), row(20)), pl.BlockSpec((L, LANES), row(21)),
                vec(3 * RWKV_WIDTH), vec(LANES), vec(LANES),
                vec(RWKV_WIDTH), mat(LANES, RWKV_WIDTH), vec(RWKV_WIDTH), mat(LANES, RWKV_WIDTH),
                mat(LANES, RWKV_WIDTH),
                vec(RWKV_WIDTH), vec(RWKV_WIDTH), vec(RWKV_WIDTH), vec(RWKV_WIDTH), vec(RWKV_WIDTH),
                mat(gw, gw), mat(L, L),
                pl.BlockSpec((L, LRU_WIDTH), row(3)), pl.BlockSpec((L, LRU_WIDTH), row(4)),
                mat(LRU_CONV, LRU_WIDTH), vec(LRU_WIDTH), mat(LRU_WIDTH, LRU_WIDTH), vec(LRU_WIDTH),
                mat(LRU_WIDTH, LRU_WIDTH), vec(LRU_WIDTH), vec(LRU_WIDTH)]
    out_row = lambda bi, i: (bi * tiles + i, 0)
    return pl.pallas_call(
        _mix0_kernel,
        grid=(bsz, tiles),
        in_specs=in_specs,
        out_specs=[pl.BlockSpec((L, RWKV_WIDTH), out_row), pl.BlockSpec((L, LRU_WIDTH), out_row)],
        out_shape=[jax.ShapeDtypeStruct((n, RWKV_WIDTH), BF16), jax.ShapeDtypeStruct((n, LRU_WIDTH), BF16)],
        scratch_shapes=[pltpu.VMEM((SUBLANES, 3 * RWKV_WIDTH), F32),
                        pltpu.VMEM((SUBLANES, LANES), F32),
                        pltpu.VMEM((SUBLANES, LANES), F32),
                        pltpu.VMEM((RWKV_HEADS // RWKV_GROUP, gw, gw), F32),
                        pltpu.VMEM((L, RWKV_WIDTH), F32),
                        pltpu.VMEM((SUBLANES, LRU_WIDTH), F32),
                        pltpu.VMEM((SUBLANES, LRU_WIDTH), F32)],
        compiler_params=_cparams(("arbitrary", "arbitrary"), 48 << 20),
    )(p, p, p, *rwkv_prm, p, p, *lru_prm)


def _conf_kernel(c_ref, cw_ref, cb_ref, g_ref, b_ref, o_ref, ext_ref):
    rows = CONF_TILE
    first = pl.program_id(1) == 0
    c = c_ref[...]
    glu = c[:, 0:CONF_WIDTH] * _sigmoid(c[:, CONF_WIDTH:2 * CONF_WIDTH])
    u = _causal_conv(ext_ref, glu, cw_ref, cb_ref, first, rows, CONF_CONV, CONF_HALO)
    mu = jnp.mean(u, axis=-1, keepdims=True)
    uc = u - mu
    var = jnp.mean(uc * uc, axis=-1, keepdims=True)
    o_ref[...] = _silu(uc * lax.rsqrt(var + LN_EPS) * g_ref[...] + b_ref[...]).astype(o_ref.dtype)


def _conf(p, prm, bsz, seq):
    rows = CONF_TILE
    tiles = seq // rows
    n = bsz * seq
    fix = lambda bi, i: (0, 0)
    vec = pl.BlockSpec((1, CONF_WIDTH), fix)
    return pl.pallas_call(
        _conf_kernel,
        grid=(bsz, tiles),
        in_specs=[pl.BlockSpec((rows, 2 * CONF_WIDTH), lambda bi, i: (bi * tiles + i, 0)),
                  pl.BlockSpec((CONF_CONV, CONF_WIDTH), fix), vec, vec, vec],
        out_specs=pl.BlockSpec((rows, CONF_WIDTH), lambda bi, i: (bi * tiles + i, 0)),
        out_shape=jax.ShapeDtypeStruct((n, CONF_WIDTH), BF16),
        scratch_shapes=[pltpu.VMEM((CONF_HALO, CONF_WIDTH), F32)],
        compiler_params=_cparams(("arbitrary", "arbitrary"), 32 << 20),
    )(p, *prm)


def _ssd_kernel(z_ref, xs_ref, bc_ref, dt_ref, cwx_ref, cbx_ref, cwb_ref, cbb_ref, dtb_ref, alog_ref,
                dskip_ref, normg_ref, expand_ref, tril_ref, o_ref, ext_xs, ext_bc, state_ref, ybuf_ref):
    L = SSM_CHUNK
    CH = SSM_STEP_CHUNKS
    rows = L * CH
    P = SSM_HEAD_DIM
    NS = SSM_STATE
    HG = SSM_HEADS // SSM_GROUPS
    first = pl.program_id(1) == 0

    @pl.when(first)
    def _():
        state_ref[...] = jnp.zeros(state_ref.shape, F32)

    xs = _silu(_causal_conv(ext_xs, xs_ref[...], cwx_ref, cbx_ref, first, rows, SSM_CONV, SUBLANES))
    bc = _silu(_causal_conv(ext_bc, bc_ref[...], cwb_ref, cbb_ref, first, rows, SSM_CONV, SUBLANES))
    dt = _softplus(dt_ref[...] + dtb_ref[...])
    a_head = -jnp.exp(alog_ref[...])
    acum = _dot_onehot_lhs(tril_ref[...], dt * (a_head * LOG2_E), 3)
    acum_x = _dot_onehot_rhs(acum, expand_ref[...], 3)
    acum_xl = jnp.concatenate(
        [jnp.broadcast_to(acum_x[c * L + L - 1:c * L + L, :], (L, SSM_INNER)) for c in range(CH)], axis=0)
    xdt = xs * _dot_onehot_rhs(dt, expand_ref[...], 3)
    x_end = (xdt * jnp.exp2(acum_xl - acum_x)).astype(BF16)
    from_start = jnp.exp2(acum_x)
    chunk_decay = jnp.exp2(acum_xl)
    causal = lax.broadcasted_iota(jnp.int32, (L, L), 1) <= lax.broadcasted_iota(jnp.int32, (L, L), 0)

    half = SSM_INNER // SSM_GROUPS
    state = state_ref[...]
    cb = {}
    for c in range(CH):
        cs = slice(c * L, (c + 1) * L)
        new_state = []
        for g in range(SSM_GROUPS):
            gs = slice(g * half, (g + 1) * half)
            bm = bc[cs, g * NS:(g + 1) * NS]
            cm = bc[cs, (SSM_GROUPS + g) * NS:(SSM_GROUPS + g + 1) * NS].astype(BF16)
            cb[c, g] = _dot_nt(cm, bm)
            ybuf_ref[cs, gs] = _dot(cm, state[:, gs]) * from_start[cs, gs]
            new_state.append(state[:, gs] * chunk_decay[c * L:c * L + 1, gs] + _dot(bm.T, x_end[cs, gs]))
        state = jnp.concatenate(new_state, axis=1)
    state_ref[...] = state
    y_off = ybuf_ref[...]
    for c in range(CH):
        cs = slice(c * L, (c + 1) * L)
        acum_c = acum[cs, :]
        acum_t = acum_c.T
        for h in range(SSM_HEADS):
            decay = jnp.exp2(jnp.where(causal, acum_c[:, h:h + 1] - acum_t[h:h + 1, :], -jnp.inf))
            ybuf_ref[cs, h * P:(h + 1) * P] = _dot(cb[c, h // HG] * decay, xdt[cs, h * P:(h + 1) * P])

    y = (ybuf_ref[...] + y_off + dskip_ref[...] * xs) * _silu(z_ref[...])
    for g in range(SSM_GROUPS):
        yg = y[:, g * half:(g + 1) * half]
        ms = jnp.mean(yg * yg, axis=-1, keepdims=True)
        o_ref[:, g * half:(g + 1) * half] = (yg * lax.rsqrt(ms + LN_EPS)
                                             * normg_ref[:, g * half:(g + 1) * half]).astype(o_ref.dtype)


def _ssd(p, prm, bsz, seq):
    L = SSM_CHUNK * SSM_STEP_CHUNKS
    tiles = seq // L
    n = bsz * seq
    row = lambda c: (lambda bi, i: (bi * tiles + i, c))
    fix = lambda bi, i: (0, 0)
    bcw = 2 * SSM_GROUPS * SSM_STATE
    vec = lambda width: pl.BlockSpec((1, width), fix)
    in_specs = [pl.BlockSpec((L, SSM_INNER), row(1)), pl.BlockSpec((L, SSM_INNER), row(2)),
                pl.BlockSpec((L, bcw), row(6)), pl.BlockSpec((L, LANES), row(28)),
                pl.BlockSpec((SSM_CONV, SSM_INNER), fix), vec(SSM_INNER),
                pl.BlockSpec((SSM_CONV, bcw), fix), vec(bcw),
                vec(LANES), vec(LANES), vec(SSM_INNER), vec(SSM_INNER),
                pl.BlockSpec((LANES, SSM_INNER), fix), pl.BlockSpec((L, L), fix)]
    return pl.pallas_call(
        _ssd_kernel,
        grid=(bsz, tiles),
        in_specs=in_specs,
        out_specs=pl.BlockSpec((L, SSM_INNER), lambda bi, i: (bi * tiles + i, 0)),
        out_shape=jax.ShapeDtypeStruct((n, SSM_INNER), BF16),
        scratch_shapes=[pltpu.VMEM((SUBLANES, SSM_INNER), F32),
                        pltpu.VMEM((SUBLANES, bcw), F32),
                        pltpu.VMEM((SSM_STATE, SSM_INNER), F32),
                        pltpu.VMEM((L, SSM_INNER), F32)],
        compiler_params=_cparams(("arbitrary", "arbitrary"), 40 << 20),
    )(p, p, p, p, *prm)


def _row(v):
    return v.reshape(1, -1).astype(F32)


def _pad_rows(m, rows, offset):
    out = jnp.zeros((rows, m.shape[1]), F32)
    return out.at[offset:offset + m.shape[0]].set(m)


def _block_diag(w):
    nb, d, e = w.shape
    eye = jnp.eye(nb, dtype=w.dtype)
    return (eye[:, None, :, None] * w[:, :, None, :]).reshape(nb * d, nb * e)


def _tril_ones(n, blocks=1):
    return jnp.asarray(np.kron(np.eye(blocks, dtype=np.float32), np.tril(np.ones((n, n), np.float32))), dtype=BF16)


def _head_sum_matrix():
    idx = np.arange(RWKV_GROUP * RWKV_HEAD_DIM) // RWKV_HEAD_DIM
    return jnp.asarray((idx[:, None] == idx[None, :]).astype(np.float32), dtype=BF16)


def _head_expand_matrix():
    m = np.zeros((LANES, SSM_INNER), np.float32)
    for h in range(SSM_HEADS):
        m[h, h * SSM_HEAD_DIM:(h + 1) * SSM_HEAD_DIM] = 1.0
    return jnp.asarray(m, dtype=BF16)


def _sublayers(h, mem2, i, bsz, xa_wq, xa_wk, xa_wv, xa_wo, ffn_w1, ffn_w2, ln_mem_g, ln_mem_b, ln_ffn_g, ln_ffn_b):
    kmem = _proj(mem2, xa_wk[i].astype(BF16), mem2.shape[0] // bsz, BF16)
    vmem_ = _proj(mem2, xa_wv[i].astype(BF16), mem2.shape[0] // bsz, BF16)
    h = _attn(h, kmem, vmem_, xa_wq[i].astype(BF16), xa_wo[i].astype(BF16), _row(ln_mem_g[i]), _row(ln_mem_b[i]),
              bsz, 1024)
    return _ffn(h, ffn_w1[i].astype(BF16), ffn_w2[i].astype(BF16), _row(ln_ffn_g[i]), _row(ln_ffn_b[i]), 512)


def kernel(x, mem, ev_w_in, ev_mu, ev_w0, ev_w2, ev_a0, ev_a2, ev_g2, ev_k_k, ev_k_a, ev_r_k, ev_gn_g, ev_gn_b, ev_lru_conv_w, ev_lru_conv_b, ev_lru_wa, ev_lru_ba, ev_lru_wx, ev_lru_bx, ev_lru_lam, ev_w_out, od_w_in, od_cf_conv_w, od_cf_conv_b, od_cf_ln_g, od_cf_ln_b, od_ssm_conv_w, od_ssm_conv_b, od_dt_bias, od_a_log, od_d_skip, od_ssm_norm_g, od_w_out, xa_wq, xa_wk, xa_wv, xa_wo, ffn_w1, ffn_w2, ln_mix_g, ln_mix_b, ln_mem_g, ln_mem_b, ln_ffn_g, ln_ffn_b):
    bsz, seq, _ = x.shape
    n = bsz * seq
    h = x.reshape(n, D_MODEL)
    mem2 = mem.reshape(bsz * mem.shape[1], D_MODEL)
    xa = (xa_wq, xa_wk, xa_wv, xa_wo, ffn_w1, ffn_w2, ln_mem_g, ln_mem_b, ln_ffn_g, ln_ffn_b)

    rw = 3 * RWKV_WIDTH
    lo = 2 * LANES
    w_in = jnp.concatenate([ev_w_in[0][:, 0:rw], ev_w_in[0][:, rw + lo:], ev_w_in[0][:, rw:rw + lo]], axis=1)
    p = _proj(h, w_in.astype(BF16), 1024)
    mu = ev_mu[0]
    rwkv_prm = (_row(mu[0:rw]), _row(mu[rw:rw + LANES]), _row(mu[rw + LANES:rw + lo]),
                _row(ev_w0[0]), _pad_rows(ev_w2[0], LANES, 0), _row(ev_a0[0]), _pad_rows(ev_a2[0], LANES, 64),
                ev_g2[0].astype(F32), _row(ev_k_k[0]), _row(ev_k_a[0]), _row(ev_r_k[0]), _row(ev_gn_g[0]),
                _row(ev_gn_b[0]), _head_sum_matrix(), _tril_ones(RWKV_CHUNK, RWKV_STEP_CHUNKS))
    lru_prm = (ev_lru_conv_w[0], _row(ev_lru_conv_b[0]), _block_diag(ev_lru_wa[0]).astype(BF16), _row(ev_lru_ba[0]),
               _block_diag(ev_lru_wx[0]).astype(BF16), _row(ev_lru_bx[0]), _row(ev_lru_lam[0]))
    y_a, y_b = _mix0(p, rwkv_prm, lru_prm, bsz, seq)
    w_out = ev_w_out[0].astype(BF16)
    h = _mix_out(y_a, y_b, w_out[0:RWKV_WIDTH], w_out[RWKV_WIDTH:], h, _row(ln_mix_g[0]), _row(ln_mix_b[0]), 1024)
    h = _sublayers(h, mem2, 0, bsz, *xa)

    w_in = jnp.pad(od_w_in[0], ((0, 0), (0, ODD_IN_PAD - ODD_IN))).astype(BF16)
    p = _proj(h, w_in, 1024)
    conf_prm = (od_cf_conv_w[0], _row(od_cf_conv_b[0]), _row(od_cf_ln_g[0]), _row(od_cf_ln_b[0]))
    y_c = _conf(p, conf_prm, bsz, seq)
    cw, cb = od_ssm_conv_w[0], od_ssm_conv_b[0]
    pad16 = lambda v: jnp.pad(v, (0, LANES - SSM_HEADS)).reshape(1, LANES).astype(F32)
    ssd_prm = (cw[:, 0:SSM_INNER], _row(cb[0:SSM_INNER]), cw[:, SSM_INNER:], _row(cb[SSM_INNER:]),
               pad16(od_dt_bias[0]), pad16(od_a_log[0]), _row(jnp.repeat(od_d_skip[0], SSM_HEAD_DIM)),
               _row(od_ssm_norm_g[0]), _head_expand_matrix(), _tril_ones(SSM_CHUNK, SSM_STEP_CHUNKS))
    y_d = _ssd(p, ssd_prm, bsz, seq)
    w_out = od_w_out[0].astype(BF16)
    h = _mix_out(y_c, y_d, w_out[0:CONF_WIDTH], w_out[CONF_WIDTH:], h, _row(ln_mix_g[1]), _row(ln_mix_b[1]), 1024)
    h = _sublayers(h, mem2, 1, bsz, *xa)
    return h.reshape(bsz, seq, D_MODEL)
```
